```python
import math
import jax, jax.numpy as jnp
from jax import lax
import numpy as np

D_MODEL = 1024
BATCH = 16
SEQ = 4096
DEPTH = 4

CHUNK = 64
N_MIXERS = 3
RMS_EPS = 1e-6

POOL_WINDOWS = (2, 4, 8, 16)
POOL_GROUPS = 4
POOL_GROUP_DIM = D_MODEL // POOL_GROUPS

GDN_HEADS = 8
GDN_HEAD_DIM = D_MODEL // GDN_HEADS
GDN_CONV = 4

FOX_HEADS = 8
FOX_HEAD_DIM = D_MODEL // FOX_HEADS
FOX_BLOCK = 128

MOE_GROUPS = 4
MOE_EXPERTS_PER_GROUP = 4
MOE_EXPERTS = MOE_GROUPS * MOE_EXPERTS_PER_GROUP
MOE_TOP_K = 2
MOE_D_FF = D_MODEL // 2

kernel_name = "hybrid_pool_gdn_fox_hmoe"


def _n_layers_of(m):
    return len(range(m, DEPTH, N_MIXERS))


def _rmsnorm(x, g):
    xf = x.astype(jnp.float32)
    y = xf * lax.rsqrt(jnp.mean(xf * xf, axis=-1, keepdims=True) + RMS_EPS)
    return (y * g.astype(jnp.float32)).astype(x.dtype)


def _l2norm(x):
    xf = x.astype(jnp.float32)
    return xf * lax.rsqrt(jnp.sum(xf * xf, axis=-1, keepdims=True) + RMS_EPS)


def _trailing_mean_minus_self(u, w):
    T = u.shape[1]
    uf = u.astype(jnp.float32)
    cs = jnp.cumsum(uf, axis=1)
    cs_shift = jnp.pad(cs, ((0, 0), (w, 0), (0, 0)))[:, :T]
    count = jnp.minimum(jnp.arange(1, T + 1), w).astype(jnp.float32)
    return ((cs - cs_shift) / count[None, :, None] - uf).astype(u.dtype)


def _pool_mixer(xn, w_in, w_grp, scale):
    B, T, D = xn.shape
    u = (xn @ w_in).reshape(B, T, POOL_GROUPS, POOL_GROUP_DIM)
    pooled = jnp.stack([_trailing_mean_minus_self(u[:, :, gi], w)
                        for gi, w in enumerate(POOL_WINDOWS)], axis=2)
    y = jnp.einsum('btgc,gce->btge', pooled, w_grp).reshape(B, T, D)
    return y * scale.astype(y.dtype)


def _causal_dwconv(u, w):
    K, C = w.shape
    return lax.conv_general_dilated(u, w[:, None, :].astype(u.dtype), window_strides=(1,),
                                    padding=[(K - 1, 0)], dimension_numbers=('NWC', 'WIO', 'NWC'),
                                    feature_group_count=C)


def _gated_delta_rule_chunked(q, k, v, g, beta):
    B, T, H, dk = q.shape
    dv = v.shape[-1]
    C = CHUNK
    N = T // C
    f32 = jnp.float32

    def blocks(a):
        a = a.astype(f32).reshape((B, N, C, H) + a.shape[3:])
        return jnp.moveaxis(a, (1, 3), (0, 2))

    q = blocks(q) * (dk ** -0.5)
    k = blocks(k)
    v = blocks(v)
    beta = blocks(beta)
    gc = jnp.cumsum(blocks(g), axis=-1)
    incl = jnp.tril(jnp.ones((C, C), bool))
    strict = jnp.tril(jnp.ones((C, C), bool), -1)
    decay = jnp.exp(jnp.where(incl, gc[..., :, None] - gc[..., None, :], -jnp.inf))
    kb = k * beta[..., None]
    L = jnp.where(strict, jnp.einsum('nbhcd,nbhsd->nbhcs', kb, k) * decay, 0.0)
    A = L + jnp.eye(C, dtype=f32)
    U = lax.linalg.triangular_solve(A, v * beta[..., None], left_side=True, lower=True,
                                    unit_diagonal=True)
    W = lax.linalg.triangular_solve(A, kb * jnp.exp(gc)[..., None], left_side=True, lower=True,
                                    unit_diagonal=True)
    Aqk = jnp.where(incl, jnp.einsum('nbhcd,nbhsd->nbhcs', q, k) * decay, 0.0)
    q_dec = q * jnp.exp(gc)[..., None]
    g_last = gc[..., -1]
    k_tail = k * jnp.exp(g_last[..., None] - gc)[..., None]

    def step(S, xs):
        q_n, kt_n, U_n, W_n, Aqk_n, gl_n = xs
        v_new = U_n - jnp.einsum('bhcd,bhde->bhce', W_n, S)
        o = jnp.einsum('bhcd,bhde->bhce', q_n, S) + jnp.einsum('bhcs,bhse->bhce', Aqk_n, v_new)
        S = S * jnp.exp(gl_n)[..., None, None] + jnp.einsum('bhcd,bhce->bhde', kt_n, v_new)
        return S, o

    S0 = jnp.zeros((B, H, dk, dv), f32)
    _, o = lax.scan(step, S0, (q_dec, k_tail, U, W, Aqk, g_last))
    return jnp.moveaxis(o, (0, 2), (1, 3)).reshape(B, T, H, dv)


def _gdn_mixer(xn, w_in, conv_w, a_log, dt_bias, norm_g, w_out):
    B, T, _ = xn.shape
    H, dh = GDN_HEADS, GDN_HEAD_DIM
    HD = H * dh
    proj = xn @ w_in
    qkv = jax.nn.silu(_causal_dwconv(proj[..., :3 * HD], conv_w))
    z = proj[..., 3 * HD:4 * HD].reshape(B, T, H, dh)
    a = proj[..., 4 * HD:4 * HD + H].astype(jnp.float32)
    b = proj[..., 4 * HD + H:].astype(jnp.float32)
    q = _l2norm(qkv[..., :HD].reshape(B, T, H, dh))
    k = _l2norm(qkv[..., HD:2 * HD].reshape(B, T, H, dh))
    v = qkv[..., 2 * HD:].reshape(B, T, H, dh)
    beta = jax.nn.sigmoid(b)
    g = -jnp.exp(a_log.astype(jnp.float32)) * jax.nn.softplus(a + dt_bias.astype(jnp.float32))
    o = _gated_delta_rule_chunked(q, k, v, g, beta)
    o = _rmsnorm(o, norm_g) * jax.nn.silu(z.astype(jnp.float32))
    return o.reshape(B, T, HD).astype(xn.dtype) @ w_out


def _fox_mixer(xn, w_in, b_f, qn_g, kn_g, w_out):
    B, T, _ = xn.shape
    H, dh = FOX_HEADS, FOX_HEAD_DIM
    HD = H * dh
    proj = xn @ w_in
    q = _rmsnorm(proj[..., :HD].reshape(B, T, H, dh), qn_g)
    k = _rmsnorm(proj[..., HD:2 * HD].reshape(B, T, H, dh), kn_g)
    v = proj[..., 2 * HD:3 * HD].reshape(B, T, H, dh)
    og = proj[..., 3 * HD:4 * HD]
    logf = jax.nn.log_sigmoid(proj[..., 4 * HD:].astype(jnp.float32) + b_f.astype(jnp.float32))
    c = jnp.transpose(jnp.cumsum(logf, axis=1), (0, 2, 1))
    qh = jnp.transpose(q, (0, 2, 1, 3))
    kh = jnp.transpose(k, (0, 2, 1, 3))
    vh = jnp.transpose(v, (0, 2, 1, 3))
    scale = dh ** -0.5
    outs = []
    for s0 in range(0, T, FOX_BLOCK):
        s1 = s0 + FOX_BLOCK
        logits = jnp.einsum('bhqd,bhkd->bhqk', qh[:, :, s0:s1], kh[:, :, :s1],
                            preferred_element_type=jnp.float32) * scale
        logits = logits + c[:, :, s0:s1, None] - c[:, :, None, :s1]
        mask = jnp.arange(s0, s1)[:, None] >= jnp.arange(s1)[None, :]
        p = jax.nn.softmax(jnp.where(mask, logits, -jnp.inf), axis=-1)
        outs.append(jnp.einsum('bhqk,bhkd->bhqd', p.astype(vh.dtype), vh[:, :, :s1]))
    o = jnp.transpose(jnp.concatenate(outs, axis=2), (0, 2, 1, 3)).reshape(B, T, HD)
    o = o * jax.nn.sigmoid(og)
    return o @ w_out


def _hier_moe(xn, w_group, b_group, w_router, b_router, w_gate, w_up, w_down):
    B, T, Dm = xn.shape
    xt = xn.reshape(-1, Dm)
    n = xt.shape[0]
    rows = jnp.arange(n)
    grp_logits = (xt @ w_group).astype(jnp.float32) + b_group.astype(jnp.float32)
    grp_prob = jax.nn.softmax(grp_logits, axis=-1)
    grp_idx = jnp.argmax(grp_logits, axis=-1)
    grp_p = grp_prob[rows, grp_idx][:, None]
    exp_logits = ((xt @ w_router).astype(jnp.float32) + b_router.astype(jnp.float32)
                  ).reshape(n, MOE_GROUPS, MOE_EXPERTS_PER_GROUP)
    in_grp = exp_logits[rows, grp_idx]
    top_val, top_idx = lax.top_k(in_grp, MOE_TOP_K)
    w = jax.nn.softmax(top_val, axis=-1) * grp_p
    glob = grp_idx[:, None] * MOE_EXPERTS_PER_GROUP + top_idx
    gate = jnp.sum(jax.nn.one_hot(glob, MOE_EXPERTS, dtype=jnp.float32) * w[..., None], axis=1)
    gate = gate.astype(xt.dtype)
    y = jnp.zeros_like(xt)
    for e in range(MOE_EXPERTS):
        h = jax.nn.silu(xt @ w_gate[e]) * (xt @ w_up[e])
        y = y + gate[:, e:e + 1] * (h @ w_down[e])
    return y.reshape(B, T, Dm)


def setup_inputs(seed: int = 0) -> dict:
    key = jax.random.key(seed)
    ks = iter(jax.random.split(key, 32))
    f32 = jnp.float32
    D = D_MODEL
    nA, nB, nC = _n_layers_of(0), _n_layers_of(1), _n_layers_of(2)
    gdn_hd = GDN_HEADS * GDN_HEAD_DIM
    fox_hd = FOX_HEADS * FOX_HEAD_DIM

    def nrm(shape, fan_in):
        return jax.random.normal(next(ks), shape, f32) * (fan_in ** -0.5)

    def gain(shape):
        return 1.0 + 0.02 * jax.random.normal(next(ks), shape, f32)

    x = jax.random.normal(next(ks), (BATCH, SEQ, D), f32)
    norm_mix_g = gain((DEPTH, D))
    norm_ffn_g = gain((DEPTH, D))
    pool_w_in = nrm((nA, D, D), D)
    pool_w_grp = nrm((nA, POOL_GROUPS, POOL_GROUP_DIM, POOL_GROUP_DIM), POOL_GROUP_DIM)
    pool_scale = gain((nA, D))
    gdn_w_in = nrm((nB, D, 4 * gdn_hd + 2 * GDN_HEADS), D)
    gdn_conv_w = nrm((nB, GDN_CONV, 3 * gdn_hd), GDN_CONV)
    gdn_a_log = jnp.log(jax.random.uniform(next(ks), (nB, GDN_HEADS), f32, 1.0, 16.0))
    dt = jnp.exp(jax.random.uniform(next(ks), (nB, GDN_HEADS), f32,
                                    math.log(1e-3), math.log(1e-1)))
    gdn_dt_bias = dt + jnp.log(-jnp.expm1(-dt))
    gdn_norm_g = gain((nB, GDN_HEAD_DIM))
    gdn_w_out = nrm((nB, gdn_hd, D), gdn_hd)
    fox_w_in = nrm((nC, D, 4 * fox_hd + FOX_HEADS), D)
    fox_b_f = jax.random.uniform(next(ks), (nC, FOX_HEADS), f32, 1.0, 4.0)
    fox_q_norm_g = gain((nC, FOX_HEAD_DIM))
    fox_k_norm_g = gain((nC, FOX_HEAD_DIM))
    fox_w_out = nrm((nC, fox_hd, D), fox_hd)
    moe_w_group = nrm((DEPTH, D, MOE_GROUPS), D)
    moe_b_group = 0.01 * jax.random.normal(next(ks), (DEPTH, MOE_GROUPS), f32)
    moe_w_router = nrm((DEPTH, D, MOE_EXPERTS), D)
    moe_b_router = 0.01 * jax.random.normal(next(ks), (DEPTH, MOE_EXPERTS), f32)
    moe_w_gate = nrm((DEPTH, MOE_EXPERTS, D, MOE_D_FF), D)
    moe_w_up = nrm((DEPTH, MOE_EXPERTS, D, MOE_D_FF), D)
    moe_w_down = nrm((DEPTH, MOE_EXPERTS, MOE_D_FF, D), MOE_D_FF)
    return {"x": x, "norm_mix_g": norm_mix_g, "norm_ffn_g": norm_ffn_g,
            "pool_w_in": pool_w_in, "pool_w_grp": pool_w_grp, "pool_scale": pool_scale,
            "gdn_w_in": gdn_w_in, "gdn_conv_w": gdn_conv_w, "gdn_a_log": gdn_a_log,
            "gdn_dt_bias": gdn_dt_bias, "gdn_norm_g": gdn_norm_g, "gdn_w_out": gdn_w_out,
            "fox_w_in": fox_w_in, "fox_b_f": fox_b_f, "fox_q_norm_g": fox_q_norm_g,
            "fox_k_norm_g": fox_k_norm_g, "fox_w_out": fox_w_out,
            "moe_w_group": moe_w_group, "moe_b_group": moe_b_group,
            "moe_w_router": moe_w_router, "moe_b_router": moe_b_router,
            "moe_w_gate": moe_w_gate, "moe_w_up": moe_w_up, "moe_w_down": moe_w_down}


def reference(x, norm_mix_g, norm_ffn_g, pool_w_in, pool_w_grp, pool_scale,
              gdn_w_in, gdn_conv_w, gdn_a_log, gdn_dt_bias, gdn_norm_g, gdn_w_out,
              fox_w_in, fox_b_f, fox_q_norm_g, fox_k_norm_g, fox_w_out,
              moe_w_group, moe_b_group, moe_w_router, moe_b_router,
              moe_w_gate, moe_w_up, moe_w_down):
    h = x
    for i in range(DEPTH):
        m, slot = i % N_MIXERS, i // N_MIXERS
        hn = _rmsnorm(h, norm_mix_g[i])
        if m == 0:
            mix = _pool_mixer(hn, pool_w_in[slot], pool_w_grp[slot], pool_scale[slot])
        elif m == 1:
            mix = _gdn_mixer(hn, gdn_w_in[slot], gdn_conv_w[slot], gdn_a_log[slot],
                             gdn_dt_bias[slot], gdn_norm_g[slot], gdn_w_out[slot])
        else:
            mix = _fox_mixer(hn, fox_w_in[slot], fox_b_f[slot], fox_q_norm_g[slot],
                             fox_k_norm_g[slot], fox_w_out[slot])
        h = h + mix.astype(h.dtype)
        h = h + _hier_moe(_rmsnorm(h, norm_ffn_g[i]), moe_w_group[i], moe_b_group[i],
                          moe_w_router[i], moe_b_router[i], moe_w_gate[i], moe_w_up[i],
                          moe_w_down[i]).astype(h.dtype)
    return h
```

```python
import functools

import jax
import jax.numpy as jnp
from jax import lax
from jax.experimental import pallas as pl
from jax.experimental.pallas import tpu as pltpu

F32 = jnp.float32
BF16 = jnp.bfloat16
I32 = jnp.int32

RMS_EPS = 1e-6
LANES = 128
VMEM_LIMIT = 56 * 1024 * 1024

POOL_WINDOWS = (2, 4, 8, 16)
POOL_HALO = 16
GDN_HEADS = 8
GDN_CONV = 4
GDN_CHUNK = 64
CONV_HALO = 8
FOX_HEADS = 8
MOE_GROUPS = 4
MOE_PER_GROUP = 4
MOE_PAIRS = 6
MOE_CLASSES = MOE_GROUPS * MOE_PAIRS
CLASS_ROWS = 32
PAIR_LO = (0, 0, 0, 1, 1, 2)
PAIR_HI = (1, 2, 3, 2, 3, 3)

TM_MIX = 256
TM_ROUTE = 256
TE = 512
GDN_TB = 256
FOX_TQ = 256
FOX_TK = 512

NT_DIMS = (((1,), (1,)), ((), ()))
TN_DIMS = (((0,), (0,)), ((), ()))


def _cparams(*sem):
    return pltpu.CompilerParams(dimension_semantics=sem, vmem_limit_bytes=VMEM_LIMIT)


def _rms(x, g):
    return x * lax.rsqrt(jnp.mean(x * x, axis=-1, keepdims=True) + RMS_EPS) * g


def _sigmoid(x):
    return 1.0 / (1.0 + jnp.exp(-x))


def _silu(x):
    return x * _sigmoid(x)


def _softplus(x):
    return jnp.maximum(x, 0.0) + jnp.log(1.0 + jnp.exp(-jnp.abs(x)))


def _bdot(a, b):
    return jnp.dot(a.astype(BF16), b.astype(BF16), preferred_element_type=F32)


def _bdot_nt(a, b):
    return lax.dot_general(a.astype(BF16), b.astype(BF16), NT_DIMS, preferred_element_type=F32)


def _head_sumsq(x, n_heads, dh):
    parts = []
    for hh in range(n_heads):
        blk = x[:, hh * dh:(hh + 1) * dh]
        s = jnp.sum(blk * blk, axis=-1, keepdims=True)
        parts.append(jnp.broadcast_to(s, blk.shape))
    return jnp.concatenate(parts, axis=-1)


def _pool_kernel(h_ref, g_ref, win_ref, wgrp_ref, scale_ref, o_ref, halo_ref, *, tm, gd):
    i = pl.program_id(1)

    @pl.when(i == 0)
    def _():
        halo_ref[...] = jnp.zeros_like(halo_ref)

    x = h_ref[...]
    xn = _rms(x, g_ref[...])
    u = _bdot(xn, win_ref[...])
    ext = jnp.concatenate([halo_ref[...], u], axis=0)
    halo_ref[...] = u[tm - POOL_HALO:, :]
    t = i * tm + lax.broadcasted_iota(I32, (tm, 1), 0)
    ys = []
    for gi, w in enumerate(POOL_WINDOWS):
        acc = ext[:, gi * gd:(gi + 1) * gd]
        span = 1
        while span < w:
            acc = acc + pltpu.roll(acc, span, 0)
            span *= 2
        cnt = jnp.minimum(t + 1, w).astype(F32)
        pooled = acc[POOL_HALO:, :] / cnt - u[:, gi * gd:(gi + 1) * gd]
        ys.append(_bdot(pooled, wgrp_ref[gi]))
    y = jnp.concatenate(ys, axis=-1) * scale_ref[...]
    o_ref[...] = x + y


def _pool_layer(h, g, w_in, w_grp, scale, B, T):
    N, D = h.shape
    tm = TM_MIX
    nt = T // tm
    gd = D // len(POOL_WINDOWS)
    row = lambda b, i: (b * nt + i, 0)
    return pl.pallas_call(
        functools.partial(_pool_kernel, tm=tm, gd=gd),
        grid=(B, nt),
        in_specs=[
            pl.BlockSpec((tm, D), row),
            pl.BlockSpec((1, D), lambda b, i: (0, 0)),
            pl.BlockSpec((D, D), lambda b, i: (0, 0)),
            pl.BlockSpec((len(POOL_WINDOWS), gd, gd), lambda b, i: (0, 0, 0)),
            pl.BlockSpec((1, D), lambda b, i: (0, 0)),
        ],
        out_specs=pl.BlockSpec((tm, D), row),
        out_shape=jax.ShapeDtypeStruct((N, D), F32),
        scratch_shapes=[pltpu.VMEM((POOL_HALO, D), F32)],
        compiler_params=_cparams("parallel", "arbitrary"),
        name="pool_mixer",
    )(h, g.reshape(1, D), w_in.astype(BF16), w_grp.astype(BF16), scale.reshape(1, D))


def _gdn_in_kernel(h_ref, g_ref, wqkv_ref, wz_ref, wab_ref, wabt_ref, conv_ref, pcol_ref, prow_ref,
                   q_ref, k_ref, v_ref, z_ref, gbc_ref, gbr_ref, halo_ref, *, tm, hd, nh):
    i = pl.program_id(1)

    @pl.when(i == 0)
    def _():
        halo_ref[...] = jnp.zeros_like(halo_ref)

    xn = _rms(h_ref[...], g_ref[...]).astype(BF16)
    pq = jnp.dot(xn, wqkv_ref[...], preferred_element_type=F32)
    ext = jnp.concatenate([halo_ref[...], pq], axis=0)
    halo_ref[...] = pq[tm - CONV_HALO:, :]
    cw = conv_ref[...]
    conv = ext * cw[GDN_CONV - 1:GDN_CONV, :]
    for j in range(1, GDN_CONV):
        conv = conv + pltpu.roll(ext, j, 0) * cw[GDN_CONV - 1 - j:GDN_CONV - j, :]
    qkv = _silu(conv[CONV_HALO:, :])
    dh = hd // nh
    q = qkv[:, :hd]
    k = qkv[:, hd:2 * hd]
    q_ref[...] = q * lax.rsqrt(_head_sumsq(q, nh, dh) + RMS_EPS)
    k_ref[...] = k * lax.rsqrt(_head_sumsq(k, nh, dh) + RMS_EPS)
    v_ref[...] = qkv[:, 2 * hd:]
    z_ref[...] = jnp.dot(xn, wz_ref[...], preferred_element_type=F32)
    ab = jnp.dot(xn, wab_ref[...], preferred_element_type=F32)
    pc = pcol_ref[...]
    lane = lax.broadcasted_iota(I32, ab.shape, 1)
    gbc_ref[...] = jnp.where(lane < nh, -pc[0:1, :] * _softplus(ab + pc[1:2, :]), _sigmoid(ab))
    abt = lax.dot_general(wabt_ref[...], xn, NT_DIMS, preferred_element_type=F32)
    pr = prow_ref[...]
    srow = lax.broadcasted_iota(I32, abt.shape, 0)
    gbr_ref[...] = jnp.where(srow < nh, -pr[:, 0:1] * _softplus(abt + pr[:, 1:2]), _sigmoid(abt))


def _gdn_chunk_kernel(q_ref, k_ref, v_ref, gbc_ref, gbr_ref, o_ref, s_ref, *, tb, nh, dh):
    @pl.when(pl.program_id(1) == 0)
    def _():
        s_ref[...] = jnp.zeros_like(s_ref)

    C = GDN_CHUNK
    ri = lax.broadcasted_iota(I32, (C, C), 0)
    ci = lax.broadcasted_iota(I32, (C, C), 1)
    incl = ri >= ci
    strict = ri > ci
    tri = incl.astype(F32)
    trit = (ri <= ci).astype(F32)
    scale = dh ** -0.5
    for c in range(tb // C):
        r0 = c * C
        gbc = gbc_ref[r0:r0 + C, :]
        gbr = gbr_ref[:, r0:r0 + C]
        for hh in range(nh):
            sl = slice(hh * dh, (hh + 1) * dh)
            q = q_ref[r0:r0 + C, sl] * scale
            k = k_ref[r0:r0 + C, sl]
            v = v_ref[r0:r0 + C, sl]
            g_col = gbc[:, hh:hh + 1]
            beta = gbc[:, nh + hh:nh + hh + 1]
            g_row = gbr[hh:hh + 1, :]
            gc_col = jnp.sum(tri * g_row, axis=1, keepdims=True)
            gc_row = jnp.sum(trit * g_col, axis=0, keepdims=True)
            decay = jnp.where(incl, jnp.exp(jnp.where(incl, gc_col - gc_row, 0.0)), 0.0)
            kb = k * beta
            kq = _bdot_nt(jnp.concatenate([kb, q], axis=0), k)
            L = jnp.where(strict, kq[:C] * decay, 0.0)
            aqk = kq[C:] * decay
            eg = jnp.exp(gc_col)
            x = jnp.concatenate([v * beta, kb * eg], axis=1)
            x = x - _bdot(L, x)
            p = L
            for _ in range(5):
                p = _bdot(p, p)
                x = x + _bdot(p, x)
            u = x[:, :dh]
            w = x[:, dh:]
            s = s_ref[hh]
            ws = _bdot(jnp.concatenate([w, q * eg], axis=0), s)
            v_new = u - ws[:C]
            o_ref[r0:r0 + C, sl] = ws[C:] + _bdot(aqk, v_new)
            g_last = gc_col[C - 1:C, :]
            k_tail = k * jnp.exp(g_last - gc_col)
            s_ref[hh] = s * jnp.exp(g_last) + lax.dot_general(
                k_tail.astype(BF16), v_new.astype(BF16), TN_DIMS, preferred_element_type=F32)


def _gdn_out_kernel(o_ref, z_ref, h_ref, ng_ref, wout_ref, out_ref, *, nh, dh):
    o = o_ref[...]
    on = o * lax.rsqrt(_head_sumsq(o, nh, dh) * (1.0 / dh) + RMS_EPS) * ng_ref[...]
    y = on * _silu(z_ref[...])
    out_ref[...] = h_ref[...] + _bdot(y, wout_ref[...])


def _gdn_layer(h, g, w_in, conv_w, a_log, dt_bias, norm_g, w_out, B, T):
    N, D = h.shape
    nh = GDN_HEADS
    hd = w_out.shape[0]
    dh = hd // nh
    tm = TM_MIX
    nt = T // tm
    w_qkv = w_in[:, :3 * hd].astype(BF16)
    w_z = w_in[:, 3 * hd:4 * hd].astype(BF16)
    w_ab = w_in[:, 4 * hd:]
    w_ab_pad = jnp.pad(w_ab, ((0, 0), (0, LANES - 2 * nh))).astype(BF16)
    w_ab_t = w_ab.T.astype(BF16)
    amul = jnp.exp(a_log.astype(F32))
    pcol = jnp.zeros((2, LANES), F32).at[0, :nh].set(amul).at[1, :nh].set(dt_bias)
    prow = jnp.zeros((2 * nh, 2), F32).at[:nh, 0].set(amul).at[:nh, 1].set(dt_bias)
    row = lambda b, i: (b * nt + i, 0)
    const2 = lambda b, i: (0, 0)
    q, k, v, z, gbc, gbr = pl.pallas_call(
        functools.partial(_gdn_in_kernel, tm=tm, hd=hd, nh=nh),
        grid=(B, nt),
        in_specs=[
            pl.BlockSpec((tm, D), row),
            pl.BlockSpec((1, D), const2),
            pl.BlockSpec((D, 3 * hd), const2),
            pl.BlockSpec((D, hd), const2),
            pl.BlockSpec((D, LANES), const2),
            pl.BlockSpec((2 * nh, D), const2),
            pl.BlockSpec((GDN_CONV, 3 * hd), const2),
            pl.BlockSpec((2, LANES), const2),
            pl.BlockSpec((2 * nh, 2), const2),
        ],
        out_specs=[
            pl.BlockSpec((tm, hd), row),
            pl.BlockSpec((tm, hd), row),
            pl.BlockSpec((tm, hd), row),
            pl.BlockSpec((tm, hd), row),
            pl.BlockSpec((tm, LANES), row),
            pl.BlockSpec((None, 2 * nh, tm), lambda b, i: (b, 0, i)),
        ],
        out_shape=[
            jax.ShapeDtypeStruct((N, hd), F32),
            jax.ShapeDtypeStruct((N, hd), F32),
            jax.ShapeDtypeStruct((N, hd), F32),
            jax.ShapeDtypeStruct((N, hd), F32),
            jax.ShapeDtypeStruct((N, LANES), F32),
            jax.ShapeDtypeStruct((B, 2 * nh, T), F32),
        ],
        scratch_shapes=[pltpu.VMEM((CONV_HALO, 3 * hd), F32)],
        compiler_params=_cparams("parallel", "arbitrary"),
        name="gdn_in",
    )(h, g.reshape(1, D), w_qkv, w_z, w_ab_pad, w_ab_t, conv_w.astype(F32), pcol, prow)

    tb = GDN_TB
    ntb = T // tb
    rowb = lambda b, i: (b * ntb + i, 0)
    o = pl.pallas_call(
        functools.partial(_gdn_chunk_kernel, tb=tb, nh=nh, dh=dh),
        grid=(B, ntb),
        in_specs=[
            pl.BlockSpec((tb, hd), rowb),
            pl.BlockSpec((tb, hd), rowb),
            pl.BlockSpec((tb, hd), rowb),
            pl.BlockSpec((tb, LANES), rowb),
            pl.BlockSpec((None, 2 * nh, tb), lambda b, i: (b, 0, i)),
        ],
        out_specs=pl.BlockSpec((tb, hd), rowb),
        out_shape=jax.ShapeDtypeStruct((N, hd), F32),
        scratch_shapes=[pltpu.VMEM((nh, dh, dh), F32)],
        compiler_params=_cparams("parallel", "arbitrary"),
        name="gdn_delta_rule",
    )(q, k, v, gbc, gbr)

    row1 = lambda i: (i, 0)
    const1 = lambda i: (0, 0)
    return pl.pallas_call(
        functools.partial(_gdn_out_kernel, nh=nh, dh=dh),
        grid=(N // tm,),
        in_specs=[
            pl.BlockSpec((tm, hd), row1),
            pl.BlockSpec((tm, hd), row1),
            pl.BlockSpec((tm, D), row1),
            pl.BlockSpec((1, hd), const1),
            pl.BlockSpec((hd, D), const1),
        ],
        out_specs=pl.BlockSpec((tm, D), row1),
        out_shape=jax.ShapeDtypeStruct((N, D), F32),
        compiler_params=_cparams("parallel"),
        name="gdn_out",
    )(o, z, h, jnp.tile(norm_g.astype(F32), nh).reshape(1, hd), w_out.astype(BF16))


def _fox_in_kernel(h_ref, g_ref, w_ref, wft_ref, bf_ref, qg_ref, kg_ref,
                   q_ref, k_ref, v_ref, og_ref, c_ref, carry_ref, *, tm, hd, nh):
    i = pl.program_id(1)

    @pl.when(i == 0)
    def _():
        carry_ref[...] = jnp.zeros_like(carry_ref)

    dh = hd // nh
    xn = _rms(h_ref[...], g_ref[...]).astype(BF16)
    pq = jnp.dot(xn, w_ref[...], preferred_element_type=F32)
    q = pq[:, :hd]
    k = pq[:, hd:2 * hd]
    q_ref[...] = (q * lax.rsqrt(_head_sumsq(q, nh, dh) * (1.0 / dh) + RMS_EPS)
                  * (qg_ref[...] * dh ** -0.5)).astype(BF16)
    k_ref[...] = (k * lax.rsqrt(_head_sumsq(k, nh, dh) * (1.0 / dh) + RMS_EPS) * kg_ref[...]).astype(BF16)
    v_ref[...] = pq[:, 2 * hd:3 * hd].astype(BF16)
    og_ref[...] = pq[:, 3 * hd:]
    ft = lax.dot_general(wft_ref[...], xn, NT_DIMS, preferred_element_type=F32) + bf_ref[...]
    logf = jnp.minimum(ft, 0.0) - jnp.log(1.0 + jnp.exp(-jnp.abs(ft)))
    upper = (lax.broadcasted_iota(I32, (tm, tm), 0) <= lax.broadcasted_iota(I32, (tm, tm), 1)).astype(F32)
    c = jnp.dot(logf, upper, preferred_element_type=F32, precision=lax.Precision.HIGHEST) + carry_ref[...]
    c_ref[...] = c
    carry_ref[...] = c[:, tm - 1:tm]


def _fox_attn_kernel(q_ref, k_ref, v_ref, c_ref, o_ref, m_ref, l_ref, acc_ref, *, tq, tk):
    qi = pl.program_id(2)
    q = q_ref[...]
    m_ref[...] = jnp.full_like(m_ref, -jnp.inf)
    l_ref[...] = jnp.zeros_like(l_ref)
    acc_ref[...] = jnp.zeros_like(acc_ref)

    def block(start, size, masked):
        kb = k_ref[pl.ds(start, size), :]
        vb = v_ref[pl.ds(start, size), :]
        s = lax.dot_general(q, kb, NT_DIMS, preferred_element_type=F32) - c_ref[:, pl.ds(start, size)]
        if masked:
            rq = qi * tq + lax.broadcasted_iota(I32, (tq, size), 0)
            ck = start + lax.broadcasted_iota(I32, (tq, size), 1)
            s = jnp.where(rq >= ck, s, -jnp.inf)
        m_old = m_ref[...]
        m_new = jnp.maximum(m_old, jnp.max(s, axis=-1, keepdims=True))
        alpha = jnp.exp(m_old - m_new)
        p = jnp.exp(s - m_new)
        l_ref[...] = l_ref[...] * alpha + jnp.sum(p, axis=-1, keepdims=True)
        acc_ref[...] = acc_ref[...] * alpha + jnp.dot(p.astype(BF16), vb, preferred_element_type=F32)
        m_ref[...] = m_new

    q0 = qi * tq
    dstart = pl.multiple_of((q0 // tk) * tk, tk)
    block(dstart, tk, True)

    def body(j, carry):
        block(pl.multiple_of(j * tk, tk), tk, False)
        return carry

    lax.fori_loop(0, q0 // tk, body, 0)
    o_ref[...] = acc_ref[...] / l_ref[...]


def _fox_out_kernel(o_ref, og_ref, h_ref, wout_ref, out_ref):
    y = o_ref[...] * _sigmoid(og_ref[...])
    out_ref[...] = h_ref[...] + _bdot(y, wout_ref[...])


def _fox_layer(h, g, w_in, b_f, qn_g, kn_g, w_out, B, T):
    N, D = h.shape
    nh = FOX_HEADS
    hd = w_out.shape[0]
    dh = hd // nh
    tm = TM_MIX
    nt = T // tm
    row = lambda b, i: (b * nt + i, 0)
    const2 = lambda b, i: (0, 0)
    q, k, v, og, c = pl.pallas_call(
        functools.partial(_fox_in_kernel, tm=tm, hd=hd, nh=nh),
        grid=(B, nt),
        in_specs=[
            pl.BlockSpec((tm, D), row),
            pl.BlockSpec((1, D), const2),
            pl.BlockSpec((D, 4 * hd), const2),
            pl.BlockSpec((nh, D), const2),
            pl.BlockSpec((nh, 1), const2),
            pl.BlockSpec((1, hd), const2),
            pl.BlockSpec((1, hd), const2),
        ],
        out_specs=[
            pl.BlockSpec((tm, hd), row),
            pl.BlockSpec((tm, hd), row),
            pl.BlockSpec((tm, hd), row),
            pl.BlockSpec((tm, hd), row),
            pl.BlockSpec((None, nh, tm), lambda b, i: (b, 0, i)),
        ],
        out_shape=[
            jax.ShapeDtypeStruct((N, hd), BF16),
            jax.ShapeDtypeStruct((N, hd), BF16),
            jax.ShapeDtypeStruct((N, hd), BF16),
            jax.ShapeDtypeStruct((N, hd), F32),
            jax.ShapeDtypeStruct((B, nh, T), F32),
        ],
        scratch_shapes=[pltpu.VMEM((nh, 1), F32)],
        compiler_params=_cparams("parallel", "arbitrary"),
        name="fox_in",
    )(h, g.reshape(1, D), w_in[:, :4 * hd].astype(BF16), w_in[:, 4 * hd:].T.astype(BF16),
      b_f.astype(F32).reshape(nh, 1), jnp.tile(qn_g.astype(F32), nh).reshape(1, hd),
      jnp.tile(kn_g.astype(F32), nh).reshape(1, hd))

    tq, tk = FOX_TQ, FOX_TK
    nq = T // tq
    o = pl.pallas_call(
        functools.partial(_fox_attn_kernel, tq=tq, tk=tk),
        grid=(B, nh, nq),
        in_specs=[
            pl.BlockSpec((tq, dh), lambda b, hh, i: (b * nq + i, hh)),
            pl.BlockSpec((T, dh), lambda b, hh, i: (b, hh)),
            pl.BlockSpec((T, dh), lambda b, hh, i: (b, hh)),
            pl.BlockSpec((None, 1, T), lambda b, hh, i: (b * nh + hh, 0, 0)),
        ],
        out_specs=pl.BlockSpec((tq, dh), lambda b, hh, i: (b * nq + i, hh)),
        out_shape=jax.ShapeDtypeStruct((N, hd), F32),
        scratch_shapes=[pltpu.VMEM((tq, 1), F32), pltpu.VMEM((tq, 1), F32), pltpu.VMEM((tq, dh), F32)],
        compiler_params=_cparams("parallel", "parallel", "arbitrary"),
        name="fox_attention",
    )(q, k, v, c.reshape(B * nh, 1, T))

    row1 = lambda i: (i, 0)
    const1 = lambda i: (0, 0)
    return pl.pallas_call(
        _fox_out_kernel,
        grid=(N // tm,),
        in_specs=[
            pl.BlockSpec((tm, hd), row1),
            pl.BlockSpec((tm, hd), row1),
            pl.BlockSpec((tm, D), row1),
            pl.BlockSpec((hd, D), const1),
        ],
        out_specs=pl.BlockSpec((tm, D), row1),
        out_shape=jax.ShapeDtypeStruct((N, D), F32),
        compiler_params=_cparams("parallel"),
        name="fox_out",
    )(o, og, h, w_out.astype(BF16))


def _router_kernel(h_ref, g_ref, w_ref, b_ref, xs_ref, meta_ref, cnt_ref, run_ref, *, tm, d):
    i = pl.program_id(0)

    @pl.when(i == 0)
    def _():
        run_ref[...] = jnp.zeros_like(run_ref)

    xn = _rms(h_ref[...], g_ref[...])
    logits = jnp.dot(xn, w_ref[...], preferred_element_type=F32,
                     precision=lax.Precision.HIGHEST) + b_ref[...]
    col = lambda j: logits[:, j:j + 1]
    gl = [col(j) for j in range(MOE_GROUPS)]
    gmax = functools.reduce(jnp.maximum, gl)
    gidx = jnp.full_like(gmax, MOE_GROUPS - 1).astype(I32)
    for j in range(MOE_GROUPS - 2, -1, -1):
        gidx = jnp.where(gl[j] == gmax, j, gidx)
    grp_p = 1.0 / functools.reduce(jnp.add, [jnp.exp(l - gmax) for l in gl])
    ev = []
    for kk in range(MOE_PER_GROUP):
        e = col(MOE_GROUPS + (MOE_GROUPS - 1) * MOE_PER_GROUP + kk)
        for gg in range(MOE_GROUPS - 2, -1, -1):
            e = jnp.where(gidx == gg, col(MOE_GROUPS + gg * MOE_PER_GROUP + kk), e)
        ev.append(e)
    m1 = functools.reduce(jnp.maximum, ev)
    i1 = jnp.full_like(gidx, MOE_PER_GROUP - 1)
    for kk in range(MOE_PER_GROUP - 2, -1, -1):
        i1 = jnp.where(ev[kk] == m1, kk, i1)
    ev2 = [jnp.where(i1 == kk, -jnp.inf, ev[kk]) for kk in range(MOE_PER_GROUP)]
    m2 = functools.reduce(jnp.maximum, ev2)
    i2 = jnp.full_like(gidx, MOE_PER_GROUP - 1)
    for kk in range(MOE_PER_GROUP - 2, -1, -1):
        i2 = jnp.where((ev2[kk] == m2) & (i1 != kk), kk, i2)
    i2 = jnp.where((i1 == MOE_PER_GROUP - 1) & (i2 == MOE_PER_GROUP - 1), MOE_PER_GROUP - 2, i2)
    e2 = jnp.exp(m2 - m1)
    w1 = grp_p / (1.0 + e2)
    w2 = grp_p * e2 / (1.0 + e2)
    lo = jnp.minimum(i1, i2)
    hi = jnp.maximum(i1, i2)
    w_lo = jnp.where(i1 < i2, w1, w2)
    w_hi = jnp.where(i1 < i2, w2, w1)
    base = jnp.where(lo == 0, 0, jnp.where(lo == 1, 3, 5))
    cls = gidx * MOE_PAIRS + base + hi - lo - 1
    lane = lax.broadcasted_iota(I32, (tm, LANES), 1)
    onehot = (lane == cls).astype(F32)
    lower = (lax.broadcasted_iota(I32, (tm, tm), 0) > lax.broadcasted_iota(I32, (tm, tm), 1))
    before = jnp.dot(lower.astype(BF16), onehot.astype(BF16), preferred_element_type=F32)
    run = run_ref[...]
    rank = jnp.sum(onehot * (before + run), axis=-1, keepdims=True)
    run = run + jnp.sum(onehot, axis=0, keepdims=True)
    run_ref[...] = run
    cnt_ref[...] = jnp.broadcast_to(run, cnt_ref.shape)
    meta = jnp.where(lane == 0, w_lo, jnp.where(lane == 1, w_hi, 0.0))
    xs_ref[...] = jnp.concatenate([xn, meta], axis=-1)
    meta_ref[...] = jnp.where(lane == 0, cls.astype(F32), jnp.where(lane == 1, rank, 0.0))


def _dispatch_kernel(cls_ref, rank_ref, off_ref, xs_ref, out_ref, sem, *, tm):
    def issue(r, carry):
        pos = off_ref[cls_ref[0, 0, r]] + rank_ref[0, 0, r]
        pltpu.make_async_copy(xs_ref.at[pl.ds(r, 1), :], out_ref.at[pl.ds(pos, 1), :], sem).start()
        return carry

    lax.fori_loop(0, tm, issue, 0, unroll=8)

    def drain(r, carry):
        pltpu.make_async_copy(xs_ref.at[pl.ds(0, 1), :], out_ref.at[pl.ds(0, 1), :], sem).wait()
        return carry

    lax.fori_loop(0, tm, drain, 0, unroll=8)


def _expert_kernel(ea_ref, eb_ref, nv_ref, x_ref, wgua_ref, wda_ref, wgub_ref, wdb_ref, y_ref, *, d, dff):
    j = pl.program_id(0)

    @pl.when(j < nv_ref[0])
    def _():
        x = x_ref[...]
        xn = x[:, :d].astype(BF16)
        wa = x[:, d:d + 1]
        wb = x[:, d + 1:d + 2]

        def expert(wgu_ref, wd_ref):
            hgu = jnp.dot(xn, wgu_ref[...], preferred_element_type=F32)
            act = _silu(hgu[:, :dff]) * hgu[:, dff:]
            return jnp.dot(act.astype(BF16), wd_ref[...], preferred_element_type=F32)

        y_ref[...] = wa * expert(wgua_ref, wda_ref) + wb * expert(wgub_ref, wdb_ref)


def _combine_kernel(cls_ref, rank_ref, off_ref, h_ref, ys_ref, out_ref, buf_ref, sem, *, tm):
    def issue(r, carry):
        pos = off_ref[cls_ref[0, 0, r]] + rank_ref[0, 0, r]
        pltpu.make_async_copy(ys_ref.at[pl.ds(pos, 1), :], buf_ref.at[pl.ds(r, 1), :], sem).start()
        return carry

    lax.fori_loop(0, tm, issue, 0, unroll=8)

    def drain(r, carry):
        pltpu.make_async_copy(ys_ref.at[pl.ds(0, 1), :], buf_ref.at[pl.ds(0, 1), :], sem).wait()
        return carry

    lax.fori_loop(0, tm, drain, 0, unroll=8)
    out_ref[...] = h_ref[...] + buf_ref[...]


def _moe_layer(h, g, w_group, b_group, w_router, b_router, w_gate, w_up, w_down):
    N, D = h.shape
    ne, _, dff = w_gate.shape
    tm = TM_ROUTE
    nb = N // tm
    n_log = MOE_GROUPS + ne
    w_r = jnp.pad(jnp.concatenate([w_group, w_router], axis=1).astype(F32), ((0, 0), (0, LANES - n_log)))
    b_r = jnp.pad(jnp.concatenate([b_group, b_router]).astype(F32), (0, LANES - n_log)).reshape(1, LANES)
    DX = D + LANES
    row1 = lambda i: (i, 0)
    const1 = lambda i: (0, 0)
    xs, meta, cnt = pl.pallas_call(
        functools.partial(_router_kernel, tm=tm, d=D),
        grid=(nb,),
        in_specs=[
            pl.BlockSpec((tm, D), row1),
            pl.BlockSpec((1, D), const1),
            pl.BlockSpec((D, LANES), const1),
            pl.BlockSpec((1, LANES), const1),
        ],
        out_specs=[
            pl.BlockSpec((tm, DX), row1),
            pl.BlockSpec((tm, LANES), row1),
            pl.BlockSpec((8, LANES), const1),
        ],
        out_shape=[
            jax.ShapeDtypeStruct((N, DX), F32),
            jax.ShapeDtypeStruct((N, LANES), F32),
            jax.ShapeDtypeStruct((8, LANES), F32),
        ],
        scratch_shapes=[pltpu.VMEM((1, LANES), F32)],
        compiler_params=_cparams("arbitrary"),
        name="moe_router",
    )(h, g.reshape(1, D), w_r, b_r)

    counts = cnt[0, :MOE_CLASSES].astype(I32)
    tiles_per = (counts + TE - 1) // TE
    tile_end = jnp.cumsum(tiles_per)
    offsets = jnp.pad((tile_end - tiles_per) * TE, (0, CLASS_ROWS - MOE_CLASSES))
    n_tiles = N // TE + MOE_CLASSES
    n_valid = tile_end[-1]
    tile_cls = jnp.searchsorted(tile_end, jnp.arange(n_tiles, dtype=I32), side="right").astype(I32)
    last_cls = jnp.searchsorted(tile_end, n_valid - 1, side="right").astype(I32)
    tile_cls = jnp.where(jnp.arange(n_tiles) < n_valid, tile_cls, last_cls)
    grp = tile_cls // MOE_PAIRS
    pid = tile_cls % MOE_PAIRS
    tile_ea = (grp * MOE_PER_GROUP + jnp.asarray(PAIR_LO, I32)[pid]).astype(I32)
    tile_eb = (grp * MOE_PER_GROUP + jnp.asarray(PAIR_HI, I32)[pid]).astype(I32)
    cls3 = meta[:, 0].astype(I32).reshape(nb, 1, tm)
    rank3 = meta[:, 1].astype(I32).reshape(nb, 1, tm)

    P = n_tiles * TE
    smem3 = pl.BlockSpec((1, 1, tm), lambda i: (i, 0, 0), memory_space=pltpu.SMEM)
    smem_all = pl.BlockSpec(memory_space=pltpu.SMEM)
    xsort = pl.pallas_call(
        functools.partial(_dispatch_kernel, tm=tm),
        grid=(nb,),
        in_specs=[smem3, smem3, smem_all, pl.BlockSpec((tm, DX), row1)],
        out_specs=pl.BlockSpec(memory_space=pl.ANY),
        out_shape=jax.ShapeDtypeStruct((P, DX), F32),
        scratch_shapes=[pltpu.SemaphoreType.DMA(())],
        compiler_params=_cparams("arbitrary"),
        name="moe_dispatch",
    )(cls3, rank3, offsets, xs)

    w_gu = jnp.concatenate([w_gate, w_up], axis=-1).astype(BF16)
    w_dn = w_down.astype(BF16)
    ysort = pl.pallas_call(
        functools.partial(_expert_kernel, d=D, dff=dff),
        grid_spec=pltpu.PrefetchScalarGridSpec(
            num_scalar_prefetch=3,
            grid=(n_tiles,),
            in_specs=[
                pl.BlockSpec((TE, DX), lambda j, ea, eb, nv: (j, 0)),
                pl.BlockSpec((None, D, 2 * dff), lambda j, ea, eb, nv: (ea[j], 0, 0)),
                pl.BlockSpec((None, dff, D), lambda j, ea, eb, nv: (ea[j], 0, 0)),
                pl.BlockSpec((None, D, 2 * dff), lambda j, ea, eb, nv: (eb[j], 0, 0)),
                pl.BlockSpec((None, dff, D), lambda j, ea, eb, nv: (eb[j], 0, 0)),
            ],
            out_specs=pl.BlockSpec((TE, D), lambda j, ea, eb, nv: (j, 0)),
        ),
        out_shape=jax.ShapeDtypeStruct((P, D), F32),
        compiler_params=_cparams("arbitrary"),
        name="moe_experts",
    )(tile_ea, tile_eb, n_valid.reshape(1).astype(I32), xsort, w_gu, w_dn, w_gu, w_dn)

    return pl.pallas_call(
        functools.partial(_combine_kernel, tm=tm),
        grid=(nb,),
        in_specs=[smem3, smem3, smem_all, pl.BlockSpec((tm, D), row1), pl.BlockSpec(memory_space=pl.ANY)],
        out_specs=pl.BlockSpec((tm, D), row1),
        out_shape=jax.ShapeDtypeStruct((N, D), F32),
        scratch_shapes=[pltpu.VMEM((tm, D), F32), pltpu.SemaphoreType.DMA(())],
        compiler_params=_cparams("arbitrary"),
        name="moe_combine",
    )(cls3, rank3, offsets, h, ysort)


def kernel(x, norm_mix_g, norm_ffn_g, pool_w_in, pool_w_grp, pool_scale, gdn_w_in, gdn_conv_w, gdn_a_log, gdn_dt_bias, gdn_norm_g, gdn_w_out, fox_w_in, fox_b_f, fox_q_norm_g, fox_k_norm_g, fox_w_out, moe_w_group, moe_b_group, moe_w_router, moe_b_router, moe_w_gate, moe_w_up, moe_w_down):
    B, T, D = x.shape
    depth = norm_mix_g.shape[0]
    h = x.reshape(B * T, D)
    for i in range(depth):
        m, slot = i % 3, i // 3
        if m == 0:
            h = _pool_layer(h, norm_mix_g[i], pool_w_in[slot], pool_w_grp[slot], pool_scale[slot], B, T)
        elif m == 1:
            h = _gdn_layer(h, norm_mix_g[i], gdn_w_in[slot], gdn_conv_w[slot], gdn_a_log[slot],
                           gdn_dt_bias[slot], gdn_norm_g[slot], gdn_w_out[slot], B, T)
        else:
            h = _fox_layer(h, norm_mix_g[i], fox_w_in[slot], fox_b_f[slot], fox_q_norm_g[slot],
                           fox_k_norm_g[slot], fox_w_out[slot], B, T)
        h = _moe_layer(h, norm_ffn_g[i], moe_w_group[i], moe_b_group[i], moe_w_router[i], moe_b_router[i],
                       moe_w_gate[i], moe_w_up[i], moe_w_down[i])
    return h.reshape(B, T, D)
```

```python
import functools

import jax
import jax.numpy as jnp
from jax import lax
from jax.experimental import pallas as pl
from jax.experimental.pallas import tpu as pltpu

F32 = jnp.float32
BF16 = jnp.bfloat16
I32 = jnp.int32

RMS_EPS = 1e-6
LANES = 128
VMEM_LIMIT = 56 * 1024 * 1024

POOL_WINDOWS = (2, 4, 8, 16)
POOL_HALO = 16
GDN_HEADS = 8
GDN_CONV = 4
GDN_CHUNK = 64
CONV_HALO = 8
FOX_HEADS = 8
MOE_GROUPS = 4
MOE_PER_GROUP = 4
MOE_PAIRS = 6
MOE_CLASSES = MOE_GROUPS * MOE_PAIRS
CLASS_ROWS = 32
PAIR_LO = (0, 0, 0, 1, 1, 2)
PAIR_HI = (1, 2, 3, 2, 3, 3)

TM_MIX = 256
TM_ROUTE = 256
TE = 512
GDN_TB = 256
FOX_TQ = 1024
FOX_SQ = 256
FOX_TK = 512
LOG2E = 1.4426950408889634

NT_DIMS = (((1,), (1,)), ((), ()))
TN_DIMS = (((0,), (0,)), ((), ()))


def _cparams(*sem):
    return pltpu.CompilerParams(dimension_semantics=sem, vmem_limit_bytes=VMEM_LIMIT)


def _rms(x, g):
    return x * lax.rsqrt(jnp.mean(x * x, axis=-1, keepdims=True) + RMS_EPS) * g


def _sigmoid(x):
    return 1.0 / (1.0 + jnp.exp(-x))


def _silu(x):
    return x * _sigmoid(x)


def _softplus(x):
    return jnp.maximum(x, 0.0) + jnp.log(1.0 + jnp.exp(-jnp.abs(x)))


def _bdot(a, b):
    return jnp.dot(a.astype(BF16), b.astype(BF16), preferred_element_type=F32)


def _bdot_nt(a, b):
    return lax.dot_general(a.astype(BF16), b.astype(BF16), NT_DIMS, preferred_element_type=F32)


def _round_robin(gens):
    alive = list(gens)
    while alive:
        for gen in list(alive):
            try:
                next(gen)
            except StopIteration:
                alive.remove(gen)


def _head_sumsq(x, n_heads, dh):
    parts = []
    for hh in range(n_heads):
        blk = x[:, hh * dh:(hh + 1) * dh]
        s = jnp.sum(blk * blk, axis=-1, keepdims=True)
        parts.append(jnp.broadcast_to(s, blk.shape))
    return jnp.concatenate(parts, axis=-1)


def _pool_kernel(h_ref, g_ref, win_ref, wgrp_ref, scale_ref, o_ref, halo_ref, *, tm, gd):
    i = pl.program_id(1)

    @pl.when(i == 0)
    def _():
        halo_ref[...] = jnp.zeros_like(halo_ref)

    x = h_ref[...]
    xn = _rms(x, g_ref[...])
    u = _bdot(xn, win_ref[...])
    ext = jnp.concatenate([halo_ref[...], u], axis=0)
    halo_ref[...] = u[tm - POOL_HALO:, :]
    t = i * tm + lax.broadcasted_iota(I32, (tm, 1), 0)
    ys = []
    for gi, w in enumerate(POOL_WINDOWS):
        acc = ext[:, gi * gd:(gi + 1) * gd]
        span = 1
        while span < w:
            acc = acc + pltpu.roll(acc, span, 0)
            span *= 2
        cnt = jnp.minimum(t + 1, w).astype(F32)
        pooled = acc[POOL_HALO:, :] / cnt - u[:, gi * gd:(gi + 1) * gd]
        ys.append(_bdot(pooled, wgrp_ref[gi]))
    y = jnp.concatenate(ys, axis=-1) * scale_ref[...]
    o_ref[...] = x + y


def _pool_layer(h, g, w_in, w_grp, scale, B, T):
    N, D = h.shape
    tm = TM_MIX
    nt = T // tm
    gd = D // len(POOL_WINDOWS)
    row = lambda b, i: (b * nt + i, 0)
    return pl.pallas_call(
        functools.partial(_pool_kernel, tm=tm, gd=gd),
        grid=(B, nt),
        in_specs=[
            pl.BlockSpec((tm, D), row),
            pl.BlockSpec((1, D), lambda b, i: (0, 0)),
            pl.BlockSpec((D, D), lambda b, i: (0, 0)),
            pl.BlockSpec((len(POOL_WINDOWS), gd, gd), lambda b, i: (0, 0, 0)),
            pl.BlockSpec((1, D), lambda b, i: (0, 0)),
        ],
        out_specs=pl.BlockSpec((tm, D), row),
        out_shape=jax.ShapeDtypeStruct((N, D), F32),
        scratch_shapes=[pltpu.VMEM((POOL_HALO, D), F32)],
        compiler_params=_cparams("parallel", "arbitrary"),
        name="pool_mixer",
    )(h, g.reshape(1, D), w_in.astype(BF16), w_grp.astype(BF16), scale.reshape(1, D))


def _gdn_in_kernel(h_ref, g_ref, wqkv_ref, wz_ref, wab_ref, wabt_ref, conv_ref, pcol_ref, prow_ref,
                   q_ref, k_ref, v_ref, z_ref, gbc_ref, gbr_ref, halo_ref, *, tm, hd, nh):
    i = pl.program_id(1)

    @pl.when(i == 0)
    def _():
        halo_ref[...] = jnp.zeros_like(halo_ref)

    xn = _rms(h_ref[...], g_ref[...]).astype(BF16)
    pq = jnp.dot(xn, wqkv_ref[...], preferred_element_type=F32)
    ext = jnp.concatenate([halo_ref[...], pq], axis=0)
    halo_ref[...] = pq[tm - CONV_HALO:, :]
    cw = conv_ref[...]
    conv = ext * cw[GDN_CONV - 1:GDN_CONV, :]
    for j in range(1, GDN_CONV):
        conv = conv + pltpu.roll(ext, j, 0) * cw[GDN_CONV - 1 - j:GDN_CONV - j, :]
    qkv = _silu(conv[CONV_HALO:, :])
    dh = hd // nh
    q = qkv[:, :hd]
    k = qkv[:, hd:2 * hd]
    q_ref[...] = q * lax.rsqrt(_head_sumsq(q, nh, dh) + RMS_EPS)
    k_ref[...] = k * lax.rsqrt(_head_sumsq(k, nh, dh) + RMS_EPS)
    v_ref[...] = qkv[:, 2 * hd:]
    z_ref[...] = jnp.dot(xn, wz_ref[...], preferred_element_type=F32)
    ab = jnp.dot(xn, wab_ref[...], preferred_element_type=F32)
    pc = pcol_ref[...]
    lane = lax.broadcasted_iota(I32, ab.shape, 1)
    gbc_ref[...] = jnp.where(lane < nh, -pc[0:1, :] * _softplus(ab + pc[1:2, :]), _sigmoid(ab))
    abt = lax.dot_general(wabt_ref[...], xn, NT_DIMS, preferred_element_type=F32)
    pr = prow_ref[...]
    srow = lax.broadcasted_iota(I32, abt.shape, 0)
    gbr_ref[...] = jnp.where(srow < nh, -pr[:, 0:1] * _softplus(abt + pr[:, 1:2]), _sigmoid(abt))


def _gdn_chunk_kernel(q_ref, k_ref, v_ref, gbc_ref, gbr_ref, o_ref, s_ref, *, tb, nh, dh):
    @pl.when(pl.program_id(1) == 0)
    def _():
        s_ref[...] = jnp.zeros_like(s_ref)

    C = GDN_CHUNK
    ri = lax.broadcasted_iota(I32, (C, C), 0)
    ci = lax.broadcasted_iota(I32, (C, C), 1)
    incl = ri >= ci
    strict = ri > ci
    tri = incl.astype(F32)
    trit = (ri <= ci).astype(F32)
    scale = dh ** -0.5
    nc = tb // C
    prep = {}
    state = [s_ref[hh] for hh in range(nh)]

    def prepare(c, hh):
        r0 = c * C
        sl = slice(hh * dh, (hh + 1) * dh)
        q = q_ref[r0:r0 + C, sl] * scale
        k = k_ref[r0:r0 + C, sl]
        v = v_ref[r0:r0 + C, sl]
        g_col = gbc_ref[r0:r0 + C, hh:hh + 1]
        beta = gbc_ref[r0:r0 + C, nh + hh:nh + hh + 1]
        g_row = gbr_ref[hh:hh + 1, r0:r0 + C]
        gc_col = jnp.sum(tri * g_row, axis=1, keepdims=True)
        gc_row = jnp.sum(trit * g_col, axis=0, keepdims=True)
        decay = jnp.where(incl, jnp.exp(jnp.where(incl, gc_col - gc_row, 0.0)), 0.0)
        kb = k * beta
        kq = _bdot_nt(jnp.concatenate([kb, q], axis=0), k)
        yield
        L = jnp.where(strict, kq[:C] * decay, 0.0)
        aqk = kq[C:] * decay
        eg = jnp.exp(gc_col)
        x = jnp.concatenate([v * beta, kb * eg], axis=1)
        lx = _bdot(L, x)
        p = _bdot(L, L)
        yield
        x = x - lx
        for it in range(5):
            px = _bdot(p, x)
            if it < 4:
                p2 = _bdot(p, p)
            yield
            x = x + px
            p = p2
        g_last = gc_col[C - 1:C, :]
        prep[(c, hh)] = dict(u=x[:, :dh], wq=jnp.concatenate([x[:, dh:], q * eg], axis=0), aqk=aqk,
                             k_tail=k * jnp.exp(g_last - gc_col), dec=jnp.exp(g_last))

    def recur(c, hh):
        r0 = c * C
        sl = slice(hh * dh, (hh + 1) * dh)
        a = prep.pop((c, hh))
        ws = _bdot(a["wq"], state[hh])
        yield
        v_new = a["u"] - ws[:C]
        ov = _bdot(a["aqk"], v_new)
        kv = lax.dot_general(a["k_tail"].astype(BF16), v_new.astype(BF16), TN_DIMS,
                             preferred_element_type=F32)
        yield
        o_ref[r0:r0 + C, sl] = ws[C:] + ov
        state[hh] = state[hh] * a["dec"] + kv

    _round_robin([prepare(0, hh) for hh in range(nh)])
    for c in range(nc):
        nxt = [prepare(c + 1, hh) for hh in range(nh)] if c + 1 < nc else []
        _round_robin([recur(c, hh) for hh in range(nh)] + nxt)
    for hh in range(nh):
        s_ref[hh] = state[hh]


def _gdn_out_kernel(o_ref, z_ref, h_ref, ng_ref, wout_ref, out_ref, *, nh, dh):
    o = o_ref[...]
    on = o * lax.rsqrt(_head_sumsq(o, nh, dh) * (1.0 / dh) + RMS_EPS) * ng_ref[...]
    y = on * _silu(z_ref[...])
    out_ref[...] = h_ref[...] + _bdot(y, wout_ref[...])


def _gdn_layer(h, g, w_in, conv_w, a_log, dt_bias, norm_g, w_out, B, T):
    N, D = h.shape
    nh = GDN_HEADS
    hd = w_out.shape[0]
    dh = hd // nh
    tm = TM_MIX
    nt = T // tm
    w_qkv = w_in[:, :3 * hd].astype(BF16)
    w_z = w_in[:, 3 * hd:4 * hd].astype(BF16)
    w_ab = w_in[:, 4 * hd:]
    w_ab_pad = jnp.pad(w_ab, ((0, 0), (0, LANES - 2 * nh))).astype(BF16)
    w_ab_t = w_ab.T.astype(BF16)
    amul = jnp.exp(a_log.astype(F32))
    pcol = jnp.zeros((2, LANES), F32).at[0, :nh].set(amul).at[1, :nh].set(dt_bias)
    prow = jnp.zeros((2 * nh, 2), F32).at[:nh, 0].set(amul).at[:nh, 1].set(dt_bias)
    row = lambda b, i: (b * nt + i, 0)
    const2 = lambda b, i: (0, 0)
    q, k, v, z, gbc, gbr = pl.pallas_call(
        functools.partial(_gdn_in_kernel, tm=tm, hd=hd, nh=nh),
        grid=(B, nt),
        in_specs=[
            pl.BlockSpec((tm, D), row),
            pl.BlockSpec((1, D), const2),
            pl.BlockSpec((D, 3 * hd), const2),
            pl.BlockSpec((D, hd), const2),
            pl.BlockSpec((D, LANES), const2),
            pl.BlockSpec((2 * nh, D), const2),
            pl.BlockSpec((GDN_CONV, 3 * hd), const2),
            pl.BlockSpec((2, LANES), const2),
            pl.BlockSpec((2 * nh, 2), const2),
        ],
        out_specs=[
            pl.BlockSpec((tm, hd), row),
            pl.BlockSpec((tm, hd), row),
            pl.BlockSpec((tm, hd), row),
            pl.BlockSpec((tm, hd), row),
            pl.BlockSpec((tm, LANES), row),
            pl.BlockSpec((None, 2 * nh, tm), lambda b, i: (b, 0, i)),
        ],
        out_shape=[
            jax.ShapeDtypeStruct((N, hd), F32),
            jax.ShapeDtypeStruct((N, hd), F32),
            jax.ShapeDtypeStruct((N, hd), F32),
            jax.ShapeDtypeStruct((N, hd), F32),
            jax.ShapeDtypeStruct((N, LANES), F32),
            jax.ShapeDtypeStruct((B, 2 * nh, T), F32),
        ],
        scratch_shapes=[pltpu.VMEM((CONV_HALO, 3 * hd), F32)],
        compiler_params=_cparams("parallel", "arbitrary"),
        name="gdn_in",
    )(h, g.reshape(1, D), w_qkv, w_z, w_ab_pad, w_ab_t, conv_w.astype(F32), pcol, prow)

    tb = GDN_TB
    ntb = T // tb
    rowb = lambda b, i: (b * ntb + i, 0)
    o = pl.pallas_call(
        functools.partial(_gdn_chunk_kernel, tb=tb, nh=nh, dh=dh),
        grid=(B, ntb),
        in_specs=[
            pl.BlockSpec((tb, hd), rowb),
            pl.BlockSpec((tb, hd), rowb),
            pl.BlockSpec((tb, hd), rowb),
            pl.BlockSpec((tb, LANES), rowb),
            pl.BlockSpec((None, 2 * nh, tb), lambda b, i: (b, 0, i)),
        ],
        out_specs=pl.BlockSpec((tb, hd), rowb),
        out_shape=jax.ShapeDtypeStruct((N, hd), F32),
        scratch_shapes=[pltpu.VMEM((nh, dh, dh), F32)],
        compiler_params=_cparams("parallel", "arbitrary"),
        name="gdn_delta_rule",
    )(q, k, v, gbc, gbr)

    row1 = lambda i: (i, 0)
    const1 = lambda i: (0, 0)
    return pl.pallas_call(
        functools.partial(_gdn_out_kernel, nh=nh, dh=dh),
        grid=(N // tm,),
        in_specs=[
            pl.BlockSpec((tm, hd), row1),
            pl.BlockSpec((tm, hd), row1),
            pl.BlockSpec((tm, D), row1),
            pl.BlockSpec((1, hd), const1),
            pl.BlockSpec((hd, D), const1),
        ],
        out_specs=pl.BlockSpec((tm, D), row1),
        out_shape=jax.ShapeDtypeStruct((N, D), F32),
        compiler_params=_cparams("parallel"),
        name="gdn_out",
    )(o, z, h, jnp.tile(norm_g.astype(F32), nh).reshape(1, hd), w_out.astype(BF16))


def _fox_in_kernel(h_ref, g_ref, w_ref, wft_ref, bf_ref, qg_ref, kg_ref,
                   q_ref, k_ref, v_ref, og_ref, c_ref, carry_ref, *, tm, hd, nh):
    i = pl.program_id(1)

    @pl.when(i == 0)
    def _():
        carry_ref[...] = jnp.zeros_like(carry_ref)

    dh = hd // nh
    xn = _rms(h_ref[...], g_ref[...]).astype(BF16)
    pq = jnp.dot(xn, w_ref[...], preferred_element_type=F32)
    q = pq[:, :hd]
    k = pq[:, hd:2 * hd]
    q_ref[...] = (q * lax.rsqrt(_head_sumsq(q, nh, dh) * (1.0 / dh) + RMS_EPS)
                  * (qg_ref[...] * (dh ** -0.5 * LOG2E))).astype(BF16)
    k_ref[...] = (k * lax.rsqrt(_head_sumsq(k, nh, dh) * (1.0 / dh) + RMS_EPS) * kg_ref[...]).astype(BF16)
    v_ref[...] = pq[:, 2 * hd:3 * hd].astype(BF16)
    og_ref[...] = pq[:, 3 * hd:]
    ft = lax.dot_general(wft_ref[...], xn, NT_DIMS, preferred_element_type=F32) + bf_ref[...]
    logf = jnp.minimum(ft, 0.0) - jnp.log(1.0 + jnp.exp(-jnp.abs(ft)))
    upper = (lax.broadcasted_iota(I32, (tm, tm), 0) <= lax.broadcasted_iota(I32, (tm, tm), 1)).astype(F32)
    c = jnp.dot(logf, upper, preferred_element_type=F32, precision=lax.Precision.HIGHEST) + carry_ref[...]
    c_ref[...] = c * LOG2E
    carry_ref[...] = c[:, tm - 1:tm]


def _fox_attn_kernel(q_ref, k_ref, v_ref, c_ref, o_ref, m_ref, l_ref, acc_ref, *, tq, sq, tk):
    ns = tq // sq
    q0 = pl.program_id(2) * tq
    m_ref[...] = jnp.full_like(m_ref, -jnp.inf)
    l_ref[...] = jnp.zeros_like(l_ref)
    acc_ref[...] = jnp.zeros_like(acc_ref)

    def stream(a, blocks):
        q = q_ref[a * sq:(a + 1) * sq, :]
        for start, size, tri in blocks:
            kb = k_ref[pl.ds(start, size), :]
            vb = v_ref[pl.ds(start, size), :]
            s = lax.dot_general(q, kb, NT_DIMS, preferred_element_type=F32) - c_ref[:, pl.ds(start, size)]
            yield
            if tri:
                s = jnp.where(lax.broadcasted_iota(I32, (sq, size), 0) >= lax.broadcasted_iota(I32, (sq, size), 1),
                              s, -jnp.inf)
            m_old = m_ref[a]
            m_new = jnp.maximum(m_old, jnp.max(s, axis=-1, keepdims=True))
            alpha = jnp.exp2(m_old - m_new)
            p = jnp.exp2(s - jnp.tile(m_new, (1, size // LANES)))
            l_new = l_ref[a] * alpha + jnp.sum(p, axis=-1, keepdims=True)
            pv = jnp.dot(p.astype(BF16), vb, preferred_element_type=F32)
            yield
            m_ref[a] = m_new
            l_ref[a] = l_new
            acc_ref[a] = acc_ref[a] * alpha + pv

    def body(j, carry):
        start = pl.multiple_of(j * tk, tk)
        _round_robin([stream(a, [(start, tk, False)]) for a in range(ns)])
        return carry

    lax.fori_loop(0, q0 // tk, body, 0)

    diag = []
    for a in range(ns):
        blocks, off = [], 0
        while off < a * sq:
            size = min(tk, a * sq - off)
            blocks.append((pl.multiple_of(q0 + off, sq), size, False))
            off += size
        blocks.append((pl.multiple_of(q0 + a * sq, sq), sq, True))
        diag.append(stream(a, blocks))
    _round_robin(diag)
    for a in range(ns):
        o_ref[a * sq:(a + 1) * sq, :] = acc_ref[a] / l_ref[a]


def _fox_out_kernel(o_ref, og_ref, h_ref, wout_ref, out_ref):
    y = o_ref[...] * _sigmoid(og_ref[...])
    out_ref[...] = h_ref[...] + _bdot(y, wout_ref[...])


def _fox_layer(h, g, w_in, b_f, qn_g, kn_g, w_out, B, T):
    N, D = h.shape
    nh = FOX_HEADS
    hd = w_out.shape[0]
    dh = hd // nh
    tm = TM_MIX
    nt = T // tm
    row = lambda b, i: (b * nt + i, 0)
    const2 = lambda b, i: (0, 0)
    q, k, v, og, c = pl.pallas_call(
        functools.partial(_fox_in_kernel, tm=tm, hd=hd, nh=nh),
        grid=(B, nt),
        in_specs=[
            pl.BlockSpec((tm, D), row),
            pl.BlockSpec((1, D), const2),
            pl.BlockSpec((D, 4 * hd), const2),
            pl.BlockSpec((nh, D), const2),
            pl.BlockSpec((nh, 1), const2),
            pl.BlockSpec((1, hd), const2),
            pl.BlockSpec((1, hd), const2),
        ],
        out_specs=[
            pl.BlockSpec((tm, hd), row),
            pl.BlockSpec((tm, hd), row),
            pl.BlockSpec((tm, hd), row),
            pl.BlockSpec((tm, hd), row),
            pl.BlockSpec((None, nh, tm), lambda b, i: (b, 0, i)),
        ],
        out_shape=[
            jax.ShapeDtypeStruct((N, hd), BF16),
            jax.ShapeDtypeStruct((N, hd), BF16),
            jax.ShapeDtypeStruct((N, hd), BF16),
            jax.ShapeDtypeStruct((N, hd), F32),
            jax.ShapeDtypeStruct((B, nh, T), F32),
        ],
        scratch_shapes=[pltpu.VMEM((nh, 1), F32)],
        compiler_params=_cparams("parallel", "arbitrary"),
        name="fox_in",
    )(h, g.reshape(1, D), w_in[:, :4 * hd].astype(BF16), w_in[:, 4 * hd:].T.astype(BF16),
      b_f.astype(F32).reshape(nh, 1), jnp.tile(qn_g.astype(F32), nh).reshape(1, hd),
      jnp.tile(kn_g.astype(F32), nh).reshape(1, hd))

    tq, sq, tk = FOX_TQ, FOX_SQ, FOX_TK
    assert dh == LANES and T % tq == 0 and tq % sq == 0 and tq % tk == 0
    nq = T // tq
    o = pl.pallas_call(
        functools.partial(_fox_attn_kernel, tq=tq, sq=sq, tk=tk),
        grid=(B, nh, nq),
        in_specs=[
            pl.BlockSpec((tq, dh), lambda b, hh, i: (b * nq + i, hh)),
            pl.BlockSpec((T, dh), lambda b, hh, i: (b, hh)),
            pl.BlockSpec((T, dh), lambda b, hh, i: (b, hh)),
            pl.BlockSpec((None, 1, T), lambda b, hh, i: (b * nh + hh, 0, 0)),
        ],
        out_specs=pl.BlockSpec((tq, dh), lambda b, hh, i: (b * nq + i, hh)),
        out_shape=jax.ShapeDtypeStruct((N, hd), F32),
        scratch_shapes=[pltpu.VMEM((tq // sq, sq, LANES), F32)] * 3,
        compiler_params=_cparams("parallel", "parallel", "arbitrary"),
        name="fox_attention",
    )(q, k, v, c.reshape(B * nh, 1, T))

    row1 = lambda i: (i, 0)
    const1 = lambda i: (0, 0)
    return pl.pallas_call(
        _fox_out_kernel,
        grid=(N // tm,),
        in_specs=[
            pl.BlockSpec((tm, hd), row1),
            pl.BlockSpec((tm, hd), row1),
            pl.BlockSpec((tm, D), row1),
            pl.BlockSpec((hd, D), const1),
        ],
        out_specs=pl.BlockSpec((tm, D), row1),
        out_shape=jax.ShapeDtypeStruct((N, D), F32),
        compiler_params=_cparams("parallel"),
        name="fox_out",
    )(o, og, h, w_out.astype(BF16))


def _router_kernel(h_ref, g_ref, wt_ref, bt_ref, xs_ref, meta_ref, cnt_ref, run_ref, *, tm, d):
    i = pl.program_id(0)

    @pl.when(i == 0)
    def _():
        run_ref[...] = jnp.zeros_like(run_ref)

    xn = _rms(h_ref[...], g_ref[...])
    logits = lax.dot_general(wt_ref[...], xn, NT_DIMS, preferred_element_type=F32,
                             precision=lax.Precision.HIGHEST) + bt_ref[...]
    col = lambda j: logits[j:j + 1, :]
    gl = [col(j) for j in range(MOE_GROUPS)]
    gmax = functools.reduce(jnp.maximum, gl)
    gidx = jnp.full_like(gmax, MOE_GROUPS - 1).astype(I32)
    for j in range(MOE_GROUPS - 2, -1, -1):
        gidx = jnp.where(gl[j] == gmax, j, gidx)
    grp_p = 1.0 / functools.reduce(jnp.add, [jnp.exp(l - gmax) for l in gl])
    ev = []
    for kk in range(MOE_PER_GROUP):
        e = col(MOE_GROUPS + (MOE_GROUPS - 1) * MOE_PER_GROUP + kk)
        for gg in range(MOE_GROUPS - 2, -1, -1):
            e = jnp.where(gidx == gg, col(MOE_GROUPS + gg * MOE_PER_GROUP + kk), e)
        ev.append(e)
    m1 = functools.reduce(jnp.maximum, ev)
    i1 = jnp.full_like(gidx, MOE_PER_GROUP - 1)
    for kk in range(MOE_PER_GROUP - 2, -1, -1):
        i1 = jnp.where(ev[kk] == m1, kk, i1)
    ev2 = [jnp.where(i1 == kk, -jnp.inf, ev[kk]) for kk in range(MOE_PER_GROUP)]
    m2 = functools.reduce(jnp.maximum, ev2)
    i2 = jnp.full_like(gidx, MOE_PER_GROUP - 1)
    for kk in range(MOE_PER_GROUP - 2, -1, -1):
        i2 = jnp.where((ev2[kk] == m2) & (i1 != kk), kk, i2)
    i2 = jnp.where((i1 == MOE_PER_GROUP - 1) & (i2 == MOE_PER_GROUP - 1), MOE_PER_GROUP - 2, i2)
    e2 = jnp.exp(m2 - m1)
    w1 = grp_p / (1.0 + e2)
    w2 = grp_p * e2 / (1.0 + e2)
    lo = jnp.minimum(i1, i2)
    hi = jnp.maximum(i1, i2)
    w_lo = jnp.where(i1 < i2, w1, w2)
    w_hi = jnp.where(i1 < i2, w2, w1)
    base = jnp.where(lo == 0, 0, jnp.where(lo == 1, 3, 5))
    cls = gidx * MOE_PAIRS + base + hi - lo - 1
    srow = lax.broadcasted_iota(I32, (CLASS_ROWS, tm), 0)
    onehot = (srow == cls).astype(F32)
    earlier = (lax.broadcasted_iota(I32, (tm, tm), 0) < lax.broadcasted_iota(I32, (tm, tm), 1))
    before = jnp.dot(onehot.astype(BF16), earlier.astype(BF16), preferred_element_type=F32)
    run = run_ref[...]
    rank = jnp.sum(onehot * (before + run), axis=0, keepdims=True)
    run = run + jnp.sum(onehot, axis=1, keepdims=True)
    run_ref[...] = run
    cnt_ref[...] = jnp.broadcast_to(run, cnt_ref.shape)
    zero = jnp.zeros_like(w_lo)
    meta_ref[...] = jnp.concatenate([w_lo, w_hi, cls.astype(F32), rank, zero, zero, zero, zero], axis=0)
    wrows = jnp.concatenate([w_lo, w_hi, jnp.zeros((LANES - 2, tm), F32)], axis=0)
    wbits = pltpu.bitcast(jnp.transpose(wrows), jnp.uint32)
    bits = pltpu.bitcast(xn.astype(BF16).astype(F32), jnp.uint32)
    half = d // 2
    packed = lax.shift_right_logical(bits[:, :half], jnp.uint32(16)) | bits[:, half:]
    xs_ref[...] = jnp.concatenate([packed, wbits], axis=-1)


def _unpack_rows(x, half):
    word = x[:, :half]
    lo = pltpu.bitcast(lax.shift_left(word, jnp.uint32(16)), F32)
    hi = pltpu.bitcast(word & jnp.uint32(0xFFFF0000), F32)
    xn = jnp.concatenate([lo, hi], axis=-1).astype(BF16)
    gates = pltpu.bitcast(x[:, half:], F32)
    return xn, gates[:, 0:1], gates[:, 1:2]


def _row_position(cls_ref, rank_ref, off_ref, r):
    return off_ref[cls_ref[0, 0, r]] + rank_ref[0, 0, r]


def _dispatch_kernel(cls_ref, rank_ref, off_ref, xs_ref, out_ref, sem, *, tm):
    copies = []
    for r in range(tm):
        pos = _row_position(cls_ref, rank_ref, off_ref, r)
        cp = pltpu.make_async_copy(xs_ref.at[pl.ds(r, 1), :], out_ref.at[pl.ds(pos, 1), :], sem)
        cp.start(priority=r % 2)
        copies.append(cp)
    for cp in copies:
        cp.wait()


def _expert_kernel(ea_ref, eb_ref, nv_ref, x_ref, wgua_ref, wda_ref, wgub_ref, wdb_ref, y_ref, *, d, dff):
    j = pl.program_id(0)

    @pl.when(j < nv_ref[0])
    def _():
        xn, wa, wb = _unpack_rows(x_ref[...], d // 2)

        def expert(wgu_ref, wd_ref):
            hgu = jnp.dot(xn, wgu_ref[...], preferred_element_type=F32)
            act = _silu(hgu[:, :dff]) * hgu[:, dff:]
            return jnp.dot(act.astype(BF16), wd_ref[...], preferred_element_type=F32)

        y_ref[...] = wa * expert(wgua_ref, wda_ref) + wb * expert(wgub_ref, wdb_ref)


def _combine_kernel(cls_ref, rank_ref, off_ref, h_ref, ys_ref, out_ref, buf_ref, sem, *, tm):
    copies = []
    for r in range(tm):
        pos = _row_position(cls_ref, rank_ref, off_ref, r)
        cp = pltpu.make_async_copy(ys_ref.at[pl.ds(pos, 1), :], buf_ref.at[pl.ds(r, 1), :], sem)
        cp.start(priority=r % 2)
        copies.append(cp)
    for cp in copies:
        cp.wait()
    out_ref[...] = h_ref[...] + buf_ref[...]


def _moe_layer(h, g, w_group, b_group, w_router, b_router, w_gate, w_up, w_down):
    N, D = h.shape
    ne, _, dff = w_gate.shape
    tm = TM_ROUTE
    nb = N // tm
    n_log = MOE_GROUPS + ne
    assert n_log <= CLASS_ROWS
    w_rt = jnp.pad(jnp.concatenate([w_group, w_router], axis=1).astype(F32).T, ((0, CLASS_ROWS - n_log), (0, 0)))
    b_rt = jnp.pad(jnp.concatenate([b_group, b_router]).astype(F32), (0, CLASS_ROWS - n_log)).reshape(CLASS_ROWS, 1)
    DX = D // 2 + LANES
    row1 = lambda i: (i, 0)
    const1 = lambda i: (0, 0)
    xs, meta, cnt = pl.pallas_call(
        functools.partial(_router_kernel, tm=tm, d=D),
        grid=(nb,),
        in_specs=[
            pl.BlockSpec((tm, D), row1),
            pl.BlockSpec((1, D), const1),
            pl.BlockSpec((CLASS_ROWS, D), const1),
            pl.BlockSpec((CLASS_ROWS, 1), const1),
        ],
        out_specs=[
            pl.BlockSpec((tm, DX), row1),
            pl.BlockSpec((8, tm), lambda i: (0, i)),
            pl.BlockSpec((CLASS_ROWS, LANES), const1),
        ],
        out_shape=[
            jax.ShapeDtypeStruct((N, DX), jnp.uint32),
            jax.ShapeDtypeStruct((8, N), F32),
            jax.ShapeDtypeStruct((CLASS_ROWS, LANES), F32),
        ],
        scratch_shapes=[pltpu.VMEM((CLASS_ROWS, 1), F32)],
        compiler_params=_cparams("arbitrary"),
        name="moe_router",
    )(h, g.reshape(1, D), w_rt, b_rt)

    counts = cnt[:MOE_CLASSES, 0].astype(I32)
    tiles_per = (counts + TE - 1) // TE
    tile_end = jnp.cumsum(tiles_per)
    offsets = jnp.pad((tile_end - tiles_per) * TE, (0, CLASS_ROWS - MOE_CLASSES))
    n_tiles = N // TE + MOE_CLASSES
    n_valid = tile_end[-1]
    tile_cls = jnp.searchsorted(tile_end, jnp.arange(n_tiles, dtype=I32), side="right").astype(I32)
    last_cls = jnp.searchsorted(tile_end, n_valid - 1, side="right").astype(I32)
    tile_cls = jnp.where(jnp.arange(n_tiles) < n_valid, tile_cls, last_cls)
    grp = tile_cls // MOE_PAIRS
    pid = tile_cls % MOE_PAIRS
    tile_ea = (grp * MOE_PER_GROUP + jnp.asarray(PAIR_LO, I32)[pid]).astype(I32)
    tile_eb = (grp * MOE_PER_GROUP + jnp.asarray(PAIR_HI, I32)[pid]).astype(I32)
    cls3 = meta[2].astype(I32).reshape(nb, 1, tm)
    rank3 = meta[3].astype(I32).reshape(nb, 1, tm)

    P = n_tiles * TE
    smem3 = pl.BlockSpec((1, 1, tm), lambda i: (i, 0, 0), memory_space=pltpu.SMEM)
    smem_all = pl.BlockSpec(memory_space=pltpu.SMEM)
    xsort = pl.pallas_call(
        functools.partial(_dispatch_kernel, tm=tm),
        grid=(nb,),
        in_specs=[smem3, smem3, smem_all, pl.BlockSpec((tm, DX), row1)],
        out_specs=pl.BlockSpec(memory_space=pl.ANY),
        out_shape=jax.ShapeDtypeStruct((P, DX), jnp.uint32),
        scratch_shapes=[pltpu.SemaphoreType.DMA(())],
        compiler_params=_cparams("arbitrary"),
        name="moe_dispatch",
    )(cls3, rank3, offsets, xs)

    w_gu = jnp.concatenate([w_gate, w_up], axis=-1).astype(BF16)
    w_dn = w_down.astype(BF16)
    ysort = pl.pallas_call(
        functools.partial(_expert_kernel, d=D, dff=dff),
        grid_spec=pltpu.PrefetchScalarGridSpec(
            num_scalar_prefetch=3,
            grid=(n_tiles,),
            in_specs=[
                pl.BlockSpec((TE, DX), lambda j, ea, eb, nv: (j, 0)),
                pl.BlockSpec((None, D, 2 * dff), lambda j, ea, eb, nv: (ea[j], 0, 0)),
                pl.BlockSpec((None, dff, D), lambda j, ea, eb, nv: (ea[j], 0, 0)),
                pl.BlockSpec((None, D, 2 * dff), lambda j, ea, eb, nv: (eb[j], 0, 0)),
                pl.BlockSpec((None, dff, D), lambda j, ea, eb, nv: (eb[j], 0, 0)),
            ],
            out_specs=pl.BlockSpec((TE, D), lambda j, ea, eb, nv: (j, 0)),
        ),
        out_shape=jax.ShapeDtypeStruct((P, D), F32),
        compiler_params=_cparams("arbitrary"),
        name="moe_experts",
    )(tile_ea, tile_eb, n_valid.reshape(1).astype(I32), xsort, w_gu, w_dn, w_gu, w_dn)

    return pl.pallas_call(
        functools.partial(_combine_kernel, tm=tm),
        grid=(nb,),
        in_specs=[smem3, smem3, smem_all, pl.BlockSpec((tm, D), row1), pl.BlockSpec(memory_space=pl.ANY)],
        out_specs=pl.BlockSpec((tm, D), row1),
        out_shape=jax.ShapeDtypeStruct((N, D), F32),
        scratch_shapes=[pltpu.VMEM((tm, D), F32), pltpu.SemaphoreType.DMA(())],
        compiler_params=_cparams("arbitrary"),
        name="moe_combine",
    )(cls3, rank3, offsets, h, ysort)


def kernel(x, norm_mix_g, norm_ffn_g, pool_w_in, pool_w_grp, pool_scale, gdn_w_in, gdn_conv_w, gdn_a_log, gdn_dt_bias, gdn_norm_g, gdn_w_out, fox_w_in, fox_b_f, fox_q_norm_g, fox_k_norm_g, fox_w_out, moe_w_group, moe_b_group, moe_w_router, moe_b_router, moe_w_gate, moe_w_up, moe_w_down):
    B, T, D = x.shape
    depth = norm_mix_g.shape[0]
    h = x.reshape(B * T, D)
    for i in range(depth):
        m, slot = i % 3, i // 3
        if m == 0:
            h = _pool_layer(h, norm_mix_g[i], pool_w_in[slot], pool_w_grp[slot], pool_scale[slot], B, T)
        elif m == 1:
            h = _gdn_layer(h, norm_mix_g[i], gdn_w_in[slot], gdn_conv_w[slot], gdn_a_log[slot],
                           gdn_dt_bias[slot], gdn_norm_g[slot], gdn_w_out[slot], B, T)
        else:
            h = _fox_layer(h, norm_mix_g[i], fox_w_in[slot], fox_b_f[slot], fox_q_norm_g[slot],
                           fox_k_norm_g[slot], fox_w_out[slot], B, T)
        h = _moe_layer(h, norm_ffn_g[i], moe_w_group[i], moe_b_group[i], moe_w_router[i], moe_b_router[i],
                       moe_w_gate[i], moe_w_up[i], moe_w_down[i])
    return h.reshape(B, T, D)
```

```python
import functools

import jax
import jax.numpy as jnp
from jax import lax
from jax.experimental import pallas as pl
from jax.experimental.pallas import tpu as pltpu

F32 = jnp.float32
BF16 = jnp.bfloat16
I32 = jnp.int32

RMS_EPS = 1e-6
LANES = 128
VMEM_LIMIT = 56 * 1024 * 1024

POOL_WINDOWS = (2, 4, 8, 16)
POOL_HALO = 16
GDN_HEADS = 8
GDN_CONV = 4
GDN_CHUNK = 64
CONV_HALO = 8
FOX_HEADS = 8
MOE_GROUPS = 4
MOE_PER_GROUP = 4
MOE_PAIRS = 6
MOE_CLASSES = MOE_GROUPS * MOE_PAIRS
CLASS_ROWS = 32
PAIR_LO = (0, 0, 0, 1, 1, 2)
PAIR_HI = (1, 2, 3, 2, 3, 3)

TM_MIX = 256
TM_OUT = 512
TM_ROUTE = 256
TM_PERMUTE = 1024
TE = 512
GDN_TB = 256
FOX_TQ = 1024
FOX_SQ = 512
FOX_TK = 512
LOG2E = 1.4426950408889634

NT_DIMS = (((1,), (1,)), ((), ()))
TN_DIMS = (((0,), (0,)), ((), ()))


def _cparams(*sem):
    return pltpu.CompilerParams(dimension_semantics=sem, vmem_limit_bytes=VMEM_LIMIT)


def _rms(x, g):
    return x * lax.rsqrt(jnp.mean(x * x, axis=-1, keepdims=True) + RMS_EPS) * g


def _sigmoid(x):
    return 1.0 / (1.0 + jnp.exp(-x))


def _silu(x):
    return x * _sigmoid(x)


def _softplus(x):
    return jnp.maximum(x, 0.0) + jnp.log(1.0 + jnp.exp(-jnp.abs(x)))


def _bdot(a, b):
    return jnp.dot(a.astype(BF16), b.astype(BF16), preferred_element_type=F32)


def _bdot_nt(a, b):
    return lax.dot_general(a.astype(BF16), b.astype(BF16), NT_DIMS, preferred_element_type=F32)


def _round_robin(gens):
    alive = list(gens)
    while alive:
        for gen in list(alive):
            try:
                next(gen)
            except StopIteration:
                alive.remove(gen)


def _head_sumsq(x, n_heads, dh):
    parts = []
    for hh in range(n_heads):
        blk = x[:, hh * dh:(hh + 1) * dh]
        s = jnp.sum(blk * blk, axis=-1, keepdims=True)
        parts.append(jnp.broadcast_to(s, blk.shape))
    return jnp.concatenate(parts, axis=-1)


def _pool_kernel(h_ref, g_ref, win_ref, wgrp_ref, scale_ref, o_ref, halo_ref, *, tm, gd):
    i = pl.program_id(1)

    @pl.when(i == 0)
    def _():
        halo_ref[...] = jnp.zeros_like(halo_ref)

    x = h_ref[...]
    xn = _rms(x, g_ref[...])
    u = _bdot(xn, win_ref[...])
    ext = jnp.concatenate([halo_ref[...], u], axis=0)
    halo_ref[...] = u[tm - POOL_HALO:, :]
    t = i * tm + lax.broadcasted_iota(I32, (tm, 1), 0)
    ys = []
    for gi, w in enumerate(POOL_WINDOWS):
        acc = ext[:, gi * gd:(gi + 1) * gd]
        span = 1
        while span < w:
            acc = acc + pltpu.roll(acc, span, 0)
            span *= 2
        cnt = jnp.minimum(t + 1, w).astype(F32)
        pooled = acc[POOL_HALO:, :] / cnt - u[:, gi * gd:(gi + 1) * gd]
        ys.append(_bdot(pooled, wgrp_ref[gi]))
    y = jnp.concatenate(ys, axis=-1) * scale_ref[...]
    o_ref[...] = x + y


def _pool_layer(h, g, w_in, w_grp, scale, B, T):
    N, D = h.shape
    tm = TM_MIX
    nt = T // tm
    gd = D // len(POOL_WINDOWS)
    row = lambda b, i: (b * nt + i, 0)
    return pl.pallas_call(
        functools.partial(_pool_kernel, tm=tm, gd=gd),
        grid=(B, nt),
        in_specs=[
            pl.BlockSpec((tm, D), row),
            pl.BlockSpec((1, D), lambda b, i: (0, 0)),
            pl.BlockSpec((D, D), lambda b, i: (0, 0)),
            pl.BlockSpec((len(POOL_WINDOWS), gd, gd), lambda b, i: (0, 0, 0)),
            pl.BlockSpec((1, D), lambda b, i: (0, 0)),
        ],
        out_specs=pl.BlockSpec((tm, D), row),
        out_shape=jax.ShapeDtypeStruct((N, D), F32),
        scratch_shapes=[pltpu.VMEM((POOL_HALO, D), F32)],
        compiler_params=_cparams("parallel", "arbitrary"),
        name="pool_mixer",
    )(h, g.reshape(1, D), w_in.astype(BF16), w_grp.astype(BF16), scale.reshape(1, D))


def _gdn_in_kernel(h_ref, g_ref, wqkv_ref, wz_ref, wab_ref, wabt_ref, conv_ref, pcol_ref, prow_ref,
                   q_ref, k_ref, v_ref, z_ref, gbc_ref, gbr_ref, halo_ref, *, tm, hd, nh):
    i = pl.program_id(1)

    @pl.when(i == 0)
    def _():
        halo_ref[...] = jnp.zeros_like(halo_ref)

    xn = _rms(h_ref[...], g_ref[...]).astype(BF16)
    pq = jnp.dot(xn, wqkv_ref[...], preferred_element_type=F32)
    ext = jnp.concatenate([halo_ref[...], pq], axis=0)
    halo_ref[...] = pq[tm - CONV_HALO:, :]
    cw = conv_ref[...]
    conv = ext * cw[GDN_CONV - 1:GDN_CONV, :]
    for j in range(1, GDN_CONV):
        conv = conv + pltpu.roll(ext, j, 0) * cw[GDN_CONV - 1 - j:GDN_CONV - j, :]
    qkv = _silu(conv[CONV_HALO:, :])
    dh = hd // nh
    q = qkv[:, :hd]
    k = qkv[:, hd:2 * hd]
    q_ref[...] = (q * lax.rsqrt(_head_sumsq(q, nh, dh) + RMS_EPS)).astype(BF16)
    k_ref[...] = (k * lax.rsqrt(_head_sumsq(k, nh, dh) + RMS_EPS)).astype(BF16)
    v_ref[...] = qkv[:, 2 * hd:].astype(BF16)
    z_ref[...] = jnp.dot(xn, wz_ref[...], preferred_element_type=F32).astype(BF16)
    ab = jnp.dot(xn, wab_ref[...], preferred_element_type=F32)
    pc = pcol_ref[...]
    lane = lax.broadcasted_iota(I32, ab.shape, 1)
    gbc_ref[...] = jnp.where(lane < nh, -pc[0:1, :] * _softplus(ab + pc[1:2, :]), _sigmoid(ab))
    abt = lax.dot_general(wabt_ref[...], xn, NT_DIMS, preferred_element_type=F32)
    pr = prow_ref[...]
    srow = lax.broadcasted_iota(I32, abt.shape, 0)
    gbr_ref[...] = jnp.where(srow < nh, -pr[:, 0:1] * _softplus(abt + pr[:, 1:2]), _sigmoid(abt))


def _gdn_chunk_kernel(q_ref, k_ref, v_ref, gbc_ref, gbr_ref, o_ref, s_ref, *, tb, nh, dh):
    @pl.when(pl.program_id(1) == 0)
    def _():
        s_ref[...] = jnp.zeros_like(s_ref)

    C = GDN_CHUNK
    ri = lax.broadcasted_iota(I32, (C, C), 0)
    ci = lax.broadcasted_iota(I32, (C, C), 1)
    incl = ri >= ci
    strict = ri > ci
    tri = incl.astype(F32)
    trit = (ri <= ci).astype(F32)
    scale = dh ** -0.5
    nc = tb // C
    prep = {}
    state = [s_ref[hh] for hh in range(nh)]

    def prepare(c, hh):
        r0 = c * C
        sl = slice(hh * dh, (hh + 1) * dh)
        q = q_ref[r0:r0 + C, sl].astype(F32) * scale
        k = k_ref[r0:r0 + C, sl].astype(F32)
        v = v_ref[r0:r0 + C, sl].astype(F32)
        g_col = gbc_ref[r0:r0 + C, hh:hh + 1]
        beta = gbc_ref[r0:r0 + C, nh + hh:nh + hh + 1]
        g_row = gbr_ref[hh:hh + 1, r0:r0 + C]
        gc_col = jnp.sum(tri * g_row, axis=1, keepdims=True)
        gc_row = jnp.sum(trit * g_col, axis=0, keepdims=True)
        decay = jnp.where(incl, jnp.exp(jnp.where(incl, gc_col - gc_row, 0.0)), 0.0)
        kb = k * beta
        kq = _bdot_nt(jnp.concatenate([kb, q], axis=0), k)
        yield
        L = jnp.where(strict, kq[:C] * decay, 0.0)
        aqk = kq[C:] * decay
        eg = jnp.exp(gc_col)
        x = jnp.concatenate([v * beta, kb * eg], axis=1)
        lx = _bdot(L, x)
        p = _bdot(L, L)
        yield
        x = x - lx
        for it in range(5):
            px = _bdot(p, x)
            if it < 4:
                p2 = _bdot(p, p)
            yield
            x = x + px
            p = p2
        g_last = gc_col[C - 1:C, :]
        prep[(c, hh)] = dict(u=x[:, :dh], wq=jnp.concatenate([x[:, dh:], q * eg], axis=0), aqk=aqk,
                             k_tail=k * jnp.exp(g_last - gc_col), dec=jnp.exp(g_last))

    def recur(c, hh):
        r0 = c * C
        sl = slice(hh * dh, (hh + 1) * dh)
        a = prep.pop((c, hh))
        ws = _bdot(a["wq"], state[hh])
        yield
        v_new = a["u"] - ws[:C]
        ov = _bdot(a["aqk"], v_new)
        kv = lax.dot_general(a["k_tail"].astype(BF16), v_new.astype(BF16), TN_DIMS,
                             preferred_element_type=F32)
        yield
        o_ref[r0:r0 + C, sl] = (ws[C:] + ov).astype(BF16)
        state[hh] = state[hh] * a["dec"] + kv

    _round_robin([prepare(0, hh) for hh in range(nh)])
    for c in range(nc):
        nxt = [prepare(c + 1, hh) for hh in range(nh)] if c + 1 < nc else []
        _round_robin([recur(c, hh) for hh in range(nh)] + nxt)
    for hh in range(nh):
        s_ref[hh] = state[hh]


def _gdn_out_kernel(o_ref, z_ref, h_ref, ng_ref, wout_ref, out_ref, *, nh, dh):
    o = o_ref[...].astype(F32)
    on = o * lax.rsqrt(_head_sumsq(o, nh, dh) * (1.0 / dh) + RMS_EPS) * ng_ref[...]
    y = on * _silu(z_ref[...].astype(F32))
    out_ref[...] = h_ref[...] + _bdot(y, wout_ref[...])


def _gdn_layer(h, g, w_in, conv_w, a_log, dt_bias, norm_g, w_out, B, T):
    N, D = h.shape
    nh = GDN_HEADS
    hd = w_out.shape[0]
    dh = hd // nh
    tm = TM_MIX
    nt = T // tm
    w_qkv = w_in[:, :3 * hd].astype(BF16)
    w_z = w_in[:, 3 * hd:4 * hd].astype(BF16)
    w_ab = w_in[:, 4 * hd:]
    w_ab_pad = jnp.pad(w_ab, ((0, 0), (0, LANES - 2 * nh))).astype(BF16)
    w_ab_t = w_ab.T.astype(BF16)
    amul = jnp.exp(a_log.astype(F32))
    pcol = jnp.zeros((2, LANES), F32).at[0, :nh].set(amul).at[1, :nh].set(dt_bias)
    prow = jnp.zeros((2 * nh, 2), F32).at[:nh, 0].set(amul).at[:nh, 1].set(dt_bias)
    row = lambda b, i: (b * nt + i, 0)
    const2 = lambda b, i: (0, 0)
    q, k, v, z, gbc, gbr = pl.pallas_call(
        functools.partial(_gdn_in_kernel, tm=tm, hd=hd, nh=nh),
        grid=(B, nt),
        in_specs=[
            pl.BlockSpec((tm, D), row),
            pl.BlockSpec((1, D), const2),
            pl.BlockSpec((D, 3 * hd), const2),
            pl.BlockSpec((D, hd), const2),
            pl.BlockSpec((D, LANES), const2),
            pl.BlockSpec((2 * nh, D), const2),
            pl.BlockSpec((GDN_CONV, 3 * hd), const2),
            pl.BlockSpec((2, LANES), const2),
            pl.BlockSpec((2 * nh, 2), const2),
        ],
        out_specs=[
            pl.BlockSpec((tm, hd), row),
            pl.BlockSpec((tm, hd), row),
            pl.BlockSpec((tm, hd), row),
            pl.BlockSpec((tm, hd), row),
            pl.BlockSpec((tm, LANES), row),
            pl.BlockSpec((None, 2 * nh, tm), lambda b, i: (b, 0, i)),
        ],
        out_shape=[
            jax.ShapeDtypeStruct((N, hd), BF16),
            jax.ShapeDtypeStruct((N, hd), BF16),
            jax.ShapeDtypeStruct((N, hd), BF16),
            jax.ShapeDtypeStruct((N, hd), BF16),
            jax.ShapeDtypeStruct((N, LANES), F32),
            jax.ShapeDtypeStruct((B, 2 * nh, T), F32),
        ],
        scratch_shapes=[pltpu.VMEM((CONV_HALO, 3 * hd), F32)],
        compiler_params=_cparams("parallel", "arbitrary"),
        name="gdn_in",
    )(h, g.reshape(1, D), w_qkv, w_z, w_ab_pad, w_ab_t, conv_w.astype(F32), pcol, prow)

    tb = GDN_TB
    ntb = T // tb
    rowb = lambda b, i: (b * ntb + i, 0)
    o = pl.pallas_call(
        functools.partial(_gdn_chunk_kernel, tb=tb, nh=nh, dh=dh),
        grid=(B, ntb),
        in_specs=[
            pl.BlockSpec((tb, hd), rowb),
            pl.BlockSpec((tb, hd), rowb),
            pl.BlockSpec((tb, hd), rowb),
            pl.BlockSpec((tb, LANES), rowb),
            pl.BlockSpec((None, 2 * nh, tb), lambda b, i: (b, 0, i)),
        ],
        out_specs=pl.BlockSpec((tb, hd), rowb),
        out_shape=jax.ShapeDtypeStruct((N, hd), BF16),
        scratch_shapes=[pltpu.VMEM((nh, dh, dh), F32)],
        compiler_params=_cparams("parallel", "arbitrary"),
        name="gdn_delta_rule",
    )(q, k, v, gbc, gbr)

    row1 = lambda i: (i, 0)
    const1 = lambda i: (0, 0)
    to = TM_OUT
    return pl.pallas_call(
        functools.partial(_gdn_out_kernel, nh=nh, dh=dh),
        grid=(N // to,),
        in_specs=[
            pl.BlockSpec((to, hd), row1),
            pl.BlockSpec((to, hd), row1),
            pl.BlockSpec((to, D), row1),
            pl.BlockSpec((1, hd), const1),
            pl.BlockSpec((hd, D), const1),
        ],
        out_specs=pl.BlockSpec((to, D), row1),
        out_shape=jax.ShapeDtypeStruct((N, D), F32),
        compiler_params=_cparams("parallel"),
        name="gdn_out",
    )(o, z, h, jnp.tile(norm_g.astype(F32), nh).reshape(1, hd), w_out.astype(BF16))


def _fox_in_kernel(h_ref, g_ref, w_ref, wft_ref, bf_ref, qg_ref, kg_ref,
                   q_ref, k_ref, v_ref, og_ref, c_ref, carry_ref, *, tm, hd, nh):
    i = pl.program_id(1)

    @pl.when(i == 0)
    def _():
        carry_ref[...] = jnp.zeros_like(carry_ref)

    dh = hd // nh
    xn = _rms(h_ref[...], g_ref[...]).astype(BF16)
    pq = jnp.dot(xn, w_ref[...], preferred_element_type=F32)
    q = pq[:, :hd]
    k = pq[:, hd:2 * hd]
    q_ref[...] = (q * lax.rsqrt(_head_sumsq(q, nh, dh) * (1.0 / dh) + RMS_EPS)
                  * (qg_ref[...] * (dh ** -0.5 * LOG2E))).astype(BF16)
    k_ref[...] = (k * lax.rsqrt(_head_sumsq(k, nh, dh) * (1.0 / dh) + RMS_EPS) * kg_ref[...]).astype(BF16)
    v_ref[...] = pq[:, 2 * hd:3 * hd].astype(BF16)
    og_ref[...] = pq[:, 3 * hd:].astype(BF16)
    ft = lax.dot_general(wft_ref[...], xn, NT_DIMS, preferred_element_type=F32) + bf_ref[...]
    logf = jnp.minimum(ft, 0.0) - jnp.log(1.0 + jnp.exp(-jnp.abs(ft)))
    upper = (lax.broadcasted_iota(I32, (tm, tm), 0) <= lax.broadcasted_iota(I32, (tm, tm), 1)).astype(F32)
    c = jnp.dot(logf, upper, preferred_element_type=F32, precision=lax.Precision.HIGHEST) + carry_ref[...]
    c_ref[...] = c * LOG2E
    carry_ref[...] = c[:, tm - 1:tm]


def _fox_attn_kernel(q_ref, k_ref, v_ref, c_ref, o_ref, m_ref, l_ref, acc_ref, *, tq, sq, tk):
    ns = tq // sq
    q0 = pl.program_id(2) * tq
    m_ref[...] = jnp.full_like(m_ref, -jnp.inf)
    l_ref[...] = jnp.zeros_like(l_ref)
    acc_ref[...] = jnp.zeros_like(acc_ref)

    def stream(a, blocks):
        q = q_ref[a * sq:(a + 1) * sq, :]
        for start, size, tri in blocks:
            kb = k_ref[pl.ds(start, size), :]
            vb = v_ref[pl.ds(start, size), :]
            s = lax.dot_general(q, kb, NT_DIMS, preferred_element_type=F32) - c_ref[:, pl.ds(start, size)]
            yield
            if tri:
                s = jnp.where(lax.broadcasted_iota(I32, (sq, size), 0) >= lax.broadcasted_iota(I32, (sq, size), 1),
                              s, -jnp.inf)
            m_old = m_ref[a]
            m_new = jnp.maximum(m_old, jnp.max(s, axis=-1, keepdims=True))
            alpha = jnp.exp2(m_old - m_new)
            p = jnp.exp2(s - jnp.tile(m_new, (1, size // LANES)))
            l_new = l_ref[a] * alpha + jnp.sum(p, axis=-1, keepdims=True)
            pv = jnp.dot(p.astype(BF16), vb, preferred_element_type=F32)
            yield
            m_ref[a] = m_new
            l_ref[a] = l_new
            acc_ref[a] = acc_ref[a] * alpha + pv

    def body(j, carry):
        start = pl.multiple_of(j * tk, tk)
        _round_robin([stream(a, [(start, tk, False)]) for a in range(ns)])
        return carry

    lax.fori_loop(0, q0 // tk, body, 0)

    diag = []
    for a in range(ns):
        blocks, off = [], 0
        while off < a * sq:
            size = min(tk, a * sq - off)
            blocks.append((pl.multiple_of(q0 + off, sq), size, False))
            off += size
        blocks.append((pl.multiple_of(q0 + a * sq, sq), sq, True))
        diag.append(stream(a, blocks))
    _round_robin(diag)
    for a in range(ns):
        o_ref[a * sq:(a + 1) * sq, :] = (acc_ref[a] / l_ref[a]).astype(BF16)


def _fox_out_kernel(o_ref, og_ref, h_ref, wout_ref, out_ref):
    y = o_ref[...].astype(F32) * _sigmoid(og_ref[...].astype(F32))
    out_ref[...] = h_ref[...] + _bdot(y, wout_ref[...])


def _fox_layer(h, g, w_in, b_f, qn_g, kn_g, w_out, B, T):
    N, D = h.shape
    nh = FOX_HEADS
    hd = w_out.shape[0]
    dh = hd // nh
    tm = TM_MIX
    nt = T // tm
    row = lambda b, i: (b * nt + i, 0)
    const2 = lambda b, i: (0, 0)
    q, k, v, og, c = pl.pallas_call(
        functools.partial(_fox_in_kernel, tm=tm, hd=hd, nh=nh),
        grid=(B, nt),
        in_specs=[
            pl.BlockSpec((tm, D), row),
            pl.BlockSpec((1, D), const2),
            pl.BlockSpec((D, 4 * hd), const2),
            pl.BlockSpec((nh, D), const2),
            pl.BlockSpec((nh, 1), const2),
            pl.BlockSpec((1, hd), const2),
            pl.BlockSpec((1, hd), const2),
        ],
        out_specs=[
            pl.BlockSpec((tm, hd), row),
            pl.BlockSpec((tm, hd), row),
            pl.BlockSpec((tm, hd), row),
            pl.BlockSpec((tm, hd), row),
            pl.BlockSpec((None, nh, tm), lambda b, i: (b, 0, i)),
        ],
        out_shape=[
            jax.ShapeDtypeStruct((N, hd), BF16),
            jax.ShapeDtypeStruct((N, hd), BF16),
            jax.ShapeDtypeStruct((N, hd), BF16),
            jax.ShapeDtypeStruct((N, hd), BF16),
            jax.ShapeDtypeStruct((B, nh, T), F32),
        ],
        scratch_shapes=[pltpu.VMEM((nh, 1), F32)],
        compiler_params=_cparams("parallel", "arbitrary"),
        name="fox_in",
    )(h, g.reshape(1, D), w_in[:, :4 * hd].astype(BF16), w_in[:, 4 * hd:].T.astype(BF16),
      b_f.astype(F32).reshape(nh, 1), jnp.tile(qn_g.astype(F32), nh).reshape(1, hd),
      jnp.tile(kn_g.astype(F32), nh).reshape(1, hd))

    tq, sq, tk = FOX_TQ, FOX_SQ, FOX_TK
    assert dh == LANES and T % tq == 0 and tq % sq == 0 and tq % tk == 0
    nq = T // tq
    o = pl.pallas_call(
        functools.partial(_fox_attn_kernel, tq=tq, sq=sq, tk=tk),
        grid=(B, nh, nq),
        in_specs=[
            pl.BlockSpec((tq, dh), lambda b, hh, i: (b * nq + i, hh)),
            pl.BlockSpec((T, dh), lambda b, hh, i: (b, hh)),
            pl.BlockSpec((T, dh), lambda b, hh, i: (b, hh)),
            pl.BlockSpec((None, 1, T), lambda b, hh, i: (b * nh + hh, 0, 0)),
        ],
        out_specs=pl.BlockSpec((tq, dh), lambda b, hh, i: (b * nq + i, hh)),
        out_shape=jax.ShapeDtypeStruct((N, hd), BF16),
        scratch_shapes=[pltpu.VMEM((tq // sq, sq, LANES), F32)] * 3,
        compiler_params=_cparams("parallel", "parallel", "arbitrary"),
        name="fox_attention",
    )(q, k, v, c.reshape(B * nh, 1, T))

    row1 = lambda i: (i, 0)
    const1 = lambda i: (0, 0)
    to = TM_OUT
    return pl.pallas_call(
        _fox_out_kernel,
        grid=(N // to,),
        in_specs=[
            pl.BlockSpec((to, hd), row1),
            pl.BlockSpec((to, hd), row1),
            pl.BlockSpec((to, D), row1),
            pl.BlockSpec((hd, D), const1),
        ],
        out_specs=pl.BlockSpec((to, D), row1),
        out_shape=jax.ShapeDtypeStruct((N, D), F32),
        compiler_params=_cparams("parallel"),
        name="fox_out",
    )(o, og, h, w_out.astype(BF16))


def _router_kernel(h_ref, g_ref, wt_ref, bt_ref, xs_ref, meta_ref, cnt_ref, run_ref, *, tm, d):
    i = pl.program_id(0)

    @pl.when(i == 0)
    def _():
        run_ref[...] = jnp.zeros_like(run_ref)

    xn = _rms(h_ref[...], g_ref[...])
    logits = lax.dot_general(wt_ref[...], xn, NT_DIMS, preferred_element_type=F32,
                             precision=lax.Precision.HIGHEST) + bt_ref[...]
    col = lambda j: logits[j:j + 1, :]
    gl = [col(j) for j in range(MOE_GROUPS)]
    gmax = functools.reduce(jnp.maximum, gl)
    gidx = jnp.full_like(gmax, MOE_GROUPS - 1).astype(I32)
    for j in range(MOE_GROUPS - 2, -1, -1):
        gidx = jnp.where(gl[j] == gmax, j, gidx)
    grp_p = 1.0 / functools.reduce(jnp.add, [jnp.exp(l - gmax) for l in gl])
    ev = []
    for kk in range(MOE_PER_GROUP):
        e = col(MOE_GROUPS + (MOE_GROUPS - 1) * MOE_PER_GROUP + kk)
        for gg in range(MOE_GROUPS - 2, -1, -1):
            e = jnp.where(gidx == gg, col(MOE_GROUPS + gg * MOE_PER_GROUP + kk), e)
        ev.append(e)
    m1 = functools.reduce(jnp.maximum, ev)
    i1 = jnp.full_like(gidx, MOE_PER_GROUP - 1)
    for kk in range(MOE_PER_GROUP - 2, -1, -1):
        i1 = jnp.where(ev[kk] == m1, kk, i1)
    ev2 = [jnp.where(i1 == kk, -jnp.inf, ev[kk]) for kk in range(MOE_PER_GROUP)]
    m2 = functools.reduce(jnp.maximum, ev2)
    i2 = jnp.full_like(gidx, MOE_PER_GROUP - 1)
    for kk in range(MOE_PER_GROUP - 2, -1, -1):
        i2 = jnp.where((ev2[kk] == m2) & (i1 != kk), kk, i2)
    i2 = jnp.where((i1 == MOE_PER_GROUP - 1) & (i2 == MOE_PER_GROUP - 1), MOE_PER_GROUP - 2, i2)
    e2 = jnp.exp(m2 - m1)
    w1 = grp_p / (1.0 + e2)
    w2 = grp_p * e2 / (1.0 + e2)
    lo = jnp.minimum(i1, i2)
    hi = jnp.maximum(i1, i2)
    w_lo = jnp.where(i1 < i2, w1, w2)
    w_hi = jnp.where(i1 < i2, w2, w1)
    base = jnp.where(lo == 0, 0, jnp.where(lo == 1, 3, 5))
    cls = gidx * MOE_PAIRS + base + hi - lo - 1
    srow = lax.broadcasted_iota(I32, (CLASS_ROWS, tm), 0)
    onehot = (srow == cls).astype(F32)
    earlier = (lax.broadcasted_iota(I32, (tm, tm), 0) < lax.broadcasted_iota(I32, (tm, tm), 1))
    before = jnp.dot(onehot.astype(BF16), earlier.astype(BF16), preferred_element_type=F32)
    run = run_ref[...]
    rank = jnp.sum(onehot * (before + run), axis=0, keepdims=True)
    run = run + jnp.sum(onehot, axis=1, keepdims=True)
    run_ref[...] = run
    cnt_ref[...] = jnp.broadcast_to(run, cnt_ref.shape)
    zero = jnp.zeros_like(w_lo)
    meta_ref[...] = jnp.concatenate([w_lo, w_hi, cls.astype(F32), rank, zero, zero, zero, zero], axis=0)
    wrows = jnp.concatenate([w_lo, w_hi, jnp.zeros((LANES - 2, tm), F32)], axis=0)
    wbits = pltpu.bitcast(jnp.transpose(wrows), jnp.uint32)
    xs_ref[...] = jnp.concatenate([_pack_bf16_pairs(xn), wbits], axis=-1)


def _pack_bf16_pairs(x):
    half = x.shape[1] // 2
    bits = pltpu.bitcast(x.astype(BF16).astype(F32), jnp.uint32)
    return lax.shift_right_logical(bits[:, :half], jnp.uint32(16)) | bits[:, half:]


def _unpack_bf16_pairs(word):
    lo = pltpu.bitcast(lax.shift_left(word, jnp.uint32(16)), F32)
    hi = pltpu.bitcast(word & jnp.uint32(0xFFFF0000), F32)
    return jnp.concatenate([lo, hi], axis=-1)


def _unpack_rows(x, half):
    gates = pltpu.bitcast(x[:, half:], F32)
    return _unpack_bf16_pairs(x[:, :half]).astype(BF16), gates[:, 0:1], gates[:, 1:2]


def _position_kernel(meta_ref, off_ref, pos_ref):
    cls = meta_ref[2:3, :].astype(I32)
    srow = lax.broadcasted_iota(I32, (CLASS_ROWS, cls.shape[1]), 0)
    off = jnp.sum(jnp.where(srow == cls, off_ref[...], 0.0), axis=0, keepdims=True)
    pos_ref[...] = (off + meta_ref[3:4, :]).astype(I32)


def _dispatch_kernel(pos_ref, xs_ref, out_ref, sem, *, tm):
    copies = []
    for r in range(tm):
        pos = pos_ref[0, 0, r]
        cp = pltpu.make_async_copy(xs_ref.at[pl.ds(r, 1), :], out_ref.at[pl.ds(pos, 1), :], sem)
        cp.start(priority=r % 2)
        copies.append(cp)
    for cp in copies:
        cp.wait()


def _expert_kernel(ea_ref, eb_ref, nv_ref, x_ref, wgua_ref, wda_ref, wgub_ref, wdb_ref, y_ref, *, d, dff):
    j = pl.program_id(0)

    @pl.when(j < nv_ref[0])
    def _():
        xn, wa, wb = _unpack_rows(x_ref[...], d // 2)

        def expert(wgu_ref, wd_ref):
            hgu = jnp.dot(xn, wgu_ref[...], preferred_element_type=F32)
            act = _silu(hgu[:, :dff]) * hgu[:, dff:]
            return jnp.dot(act.astype(BF16), wd_ref[...], preferred_element_type=F32)

        y_ref[...] = _pack_bf16_pairs(wa * expert(wgua_ref, wda_ref) + wb * expert(wgub_ref, wdb_ref))


def _combine_kernel(pos_ref, h_ref, ys_ref, out_ref, buf_ref, sem, *, tm):
    copies = []
    for r in range(tm):
        pos = pos_ref[0, 0, r]
        cp = pltpu.make_async_copy(ys_ref.at[pl.ds(pos, 1), :], buf_ref.at[pl.ds(r, 1), :], sem)
        cp.start(priority=r % 2)
        copies.append(cp)
    for cp in copies:
        cp.wait()
    out_ref[...] = h_ref[...] + _unpack_bf16_pairs(buf_ref[...])


def _moe_layer(h, g, w_group, b_group, w_router, b_router, w_gate, w_up, w_down):
    N, D = h.shape
    ne, _, dff = w_gate.shape
    tm = TM_ROUTE
    nb = N // tm
    n_log = MOE_GROUPS + ne
    assert n_log <= CLASS_ROWS
    w_rt = jnp.pad(jnp.concatenate([w_group, w_router], axis=1).astype(F32).T, ((0, CLASS_ROWS - n_log), (0, 0)))
    b_rt = jnp.pad(jnp.concatenate([b_group, b_router]).astype(F32), (0, CLASS_ROWS - n_log)).reshape(CLASS_ROWS, 1)
    DX = D // 2 + LANES
    row1 = lambda i: (i, 0)
    const1 = lambda i: (0, 0)
    xs, meta, cnt = pl.pallas_call(
        functools.partial(_router_kernel, tm=tm, d=D),
        grid=(nb,),
        in_specs=[
            pl.BlockSpec((tm, D), row1),
            pl.BlockSpec((1, D), const1),
            pl.BlockSpec((CLASS_ROWS, D), const1),
            pl.BlockSpec((CLASS_ROWS, 1), const1),
        ],
        out_specs=[
            pl.BlockSpec((tm, DX), row1),
            pl.BlockSpec((8, tm), lambda i: (0, i)),
            pl.BlockSpec((CLASS_ROWS, LANES), const1),
        ],
        out_shape=[
            jax.ShapeDtypeStruct((N, DX), jnp.uint32),
            jax.ShapeDtypeStruct((8, N), F32),
            jax.ShapeDtypeStruct((CLASS_ROWS, LANES), F32),
        ],
        scratch_shapes=[pltpu.VMEM((CLASS_ROWS, 1), F32)],
        compiler_params=_cparams("arbitrary"),
        name="moe_router",
    )(h, g.reshape(1, D), w_rt, b_rt)

    counts = cnt[:MOE_CLASSES, 0].astype(I32)
    tiles_per = (counts + TE - 1) // TE
    tile_end = jnp.cumsum(tiles_per)
    offsets = jnp.pad((tile_end - tiles_per) * TE, (0, CLASS_ROWS - MOE_CLASSES))
    n_tiles = N // TE + MOE_CLASSES
    n_valid = tile_end[-1]
    tile_id = jnp.minimum(jnp.arange(n_tiles, dtype=I32), n_valid - 1)
    tile_cls = jnp.sum((tile_end[None, :] <= tile_id[:, None]).astype(I32), axis=1)
    grp = tile_cls // MOE_PAIRS
    pid = tile_cls % MOE_PAIRS
    tile_ea = (grp * MOE_PER_GROUP + jnp.asarray(PAIR_LO, I32)[pid]).astype(I32)
    tile_eb = (grp * MOE_PER_GROUP + jnp.asarray(PAIR_HI, I32)[pid]).astype(I32)
    tp = TM_PERMUTE
    npb = N // tp
    tpos = min(8192, N)
    pos3 = pl.pallas_call(
        _position_kernel,
        grid=(N // tpos,),
        in_specs=[pl.BlockSpec((8, tpos), lambda i: (0, i)), pl.BlockSpec((CLASS_ROWS, 1), const1)],
        out_specs=pl.BlockSpec((1, tpos), lambda i: (0, i)),
        out_shape=jax.ShapeDtypeStruct((1, N), I32),
        compiler_params=_cparams("parallel"),
        name="moe_positions",
    )(meta, offsets.astype(F32).reshape(CLASS_ROWS, 1)).reshape(npb, 1, tp)

    P = n_tiles * TE
    smem3 = pl.BlockSpec((1, 1, tp), lambda i: (i, 0, 0), memory_space=pltpu.SMEM)
    xsort = pl.pallas_call(
        functools.partial(_dispatch_kernel, tm=tp),
        grid=(npb,),
        in_specs=[smem3, pl.BlockSpec((tp, DX), row1)],
        out_specs=pl.BlockSpec(memory_space=pl.ANY),
        out_shape=jax.ShapeDtypeStruct((P, DX), jnp.uint32),
        scratch_shapes=[pltpu.SemaphoreType.DMA(())],
        compiler_params=_cparams("arbitrary"),
        name="moe_dispatch",
    )(pos3, xs)

    w_gu = jnp.concatenate([w_gate, w_up], axis=-1).astype(BF16)
    w_dn = w_down.astype(BF16)
    ysort = pl.pallas_call(
        functools.partial(_expert_kernel, d=D, dff=dff),
        grid_spec=pltpu.PrefetchScalarGridSpec(
            num_scalar_prefetch=3,
            grid=(n_tiles,),
            in_specs=[
                pl.BlockSpec((TE, DX), lambda j, ea, eb, nv: (j, 0)),
                pl.BlockSpec((None, D, 2 * dff), lambda j, ea, eb, nv: (ea[j], 0, 0)),
                pl.BlockSpec((None, dff, D), lambda j, ea, eb, nv: (ea[j], 0, 0)),
                pl.BlockSpec((None, D, 2 * dff), lambda j, ea, eb, nv: (eb[j], 0, 0)),
                pl.BlockSpec((None, dff, D), lambda j, ea, eb, nv: (eb[j], 0, 0)),
            ],
            out_specs=pl.BlockSpec((TE, D // 2), lambda j, ea, eb, nv: (j, 0)),
        ),
        out_shape=jax.ShapeDtypeStruct((P, D // 2), jnp.uint32),
        compiler_params=_cparams("arbitrary"),
        name="moe_experts",
    )(tile_ea, tile_eb, n_valid.reshape(1).astype(I32), xsort, w_gu, w_dn, w_gu, w_dn)

    return pl.pallas_call(
        functools.partial(_combine_kernel, tm=tp),
        grid=(npb,),
        in_specs=[smem3, pl.BlockSpec((tp, D), row1), pl.BlockSpec(memory_space=pl.ANY)],
        out_specs=pl.BlockSpec((tp, D), row1),
        out_shape=jax.ShapeDtypeStruct((N, D), F32),
        scratch_shapes=[pltpu.VMEM((tp, D // 2), jnp.uint32), pltpu.SemaphoreType.DMA(())],
        compiler_params=_cparams("arbitrary"),
        name="moe_combine",
    )(pos3, h, ysort)


def kernel(x, norm_mix_g, norm_ffn_g, pool_w_in, pool_w_grp, pool_scale, gdn_w_in, gdn_conv_w, gdn_a_log, gdn_dt_bias, gdn_norm_g, gdn_w_out, fox_w_in, fox_b_f, fox_q_norm_g, fox_k_norm_g, fox_w_out, moe_w_group, moe_b_group, moe_w_router, moe_b_router, moe_w_gate, moe_w_up, moe_w_down):
    B, T, D = x.shape
    depth = norm_mix_g.shape[0]
    h = x.reshape(B * T, D)
    for i in range(depth):
        m, slot = i % 3, i // 3
        if m == 0:
            h = _pool_layer(h, norm_mix_g[i], pool_w_in[slot], pool_w_grp[slot], pool_scale[slot], B, T)
        elif m == 1:
            h = _gdn_layer(h, norm_mix_g[i], gdn_w_in[slot], gdn_conv_w[slot], gdn_a_log[slot],
                           gdn_dt_bias[slot], gdn_norm_g[slot], gdn_w_out[slot], B, T)
        else:
            h = _fox_layer(h, norm_mix_g[i], fox_w_in[slot], fox_b_f[slot], fox_q_norm_g[slot],
                           fox_k_norm_g[slot], fox_w_out[slot], B, T)
        h = _moe_layer(h, norm_ffn_g[i], moe_w_group[i], moe_b_group[i], moe_w_router[i], moe_b_router[i],
                       moe_w_gate[i], moe_w_up[i], moe_w_down[i])
    return h.reshape(B, T, D)
```

```python
import functools

import jax
import jax.numpy as jnp
from jax import lax
from jax.experimental import pallas as pl
from jax.experimental.pallas import tpu as pltpu

F32 = jnp.float32
BF16 = jnp.bfloat16
I32 = jnp.int32

RMS_EPS = 1e-6
LANES = 128
VMEM_LIMIT = 56 * 1024 * 1024

POOL_WINDOWS = (2, 4, 8, 16)
POOL_HALO = 16
GDN_HEADS = 8
GDN_CONV = 4
GDN_CHUNK = 64
CONV_HALO = 8
FOX_HEADS = 8
MOE_GROUPS = 4
MOE_PER_GROUP = 4
MOE_PAIRS = 6
MOE_CLASSES = MOE_GROUPS * MOE_PAIRS
CLASS_ROWS = 32
PAIR_LO = (0, 0, 0, 1, 1, 2)
PAIR_HI = (1, 2, 3, 2, 3, 3)

TM_MIX = 256
TM_OUT = 512
TM_ROUTE = 512
TM_PERMUTE = 1024
TE = 512
GDN_TB = 512
FOX_TQ = 1024
FOX_SQ = 512
FOX_TK = 1024
LOG2E = 1.4426950408889634

NT_DIMS = (((1,), (1,)), ((), ()))
TN_DIMS = (((0,), (0,)), ((), ()))


def _cparams(*sem):
    return pltpu.CompilerParams(dimension_semantics=sem, vmem_limit_bytes=VMEM_LIMIT)


def _rms(x, g):
    return x * lax.rsqrt(jnp.mean(x * x, axis=-1, keepdims=True) + RMS_EPS) * g


def _sigmoid(x):
    return 1.0 / (1.0 + jnp.exp(-x))


def _silu(x):
    return x * _sigmoid(x)


def _softplus(x):
    return jnp.maximum(x, 0.0) + jnp.log(1.0 + jnp.exp(-jnp.abs(x)))


def _bdot(a, b):
    return jnp.dot(a.astype(BF16), b.astype(BF16), preferred_element_type=F32)


def _bdot_nt(a, b):
    return lax.dot_general(a.astype(BF16), b.astype(BF16), NT_DIMS, preferred_element_type=F32)


def _round_robin(gens):
    alive = list(gens)
    while alive:
        for gen in list(alive):
            try:
                next(gen)
            except StopIteration:
                alive.remove(gen)


def _head_sumsq(x, n_heads, dh):
    parts = []
    for hh in range(n_heads):
        blk = x[:, hh * dh:(hh + 1) * dh]
        s = jnp.sum(blk * blk, axis=-1, keepdims=True)
        parts.append(jnp.broadcast_to(s, blk.shape))
    return jnp.concatenate(parts, axis=-1)


def _pool_kernel(h_ref, g_ref, win_ref, wgrp_ref, scale_ref, o_ref, halo_ref, *, tm, gd):
    i = pl.program_id(1)

    @pl.when(i == 0)
    def _():
        halo_ref[...] = jnp.zeros_like(halo_ref)

    x = h_ref[...]
    xn = _rms(x, g_ref[...])
    u = _bdot(xn, win_ref[...])
    ext = jnp.concatenate([halo_ref[...], u], axis=0)
    halo_ref[...] = u[tm - POOL_HALO:, :]
    t = i * tm + lax.broadcasted_iota(I32, (tm, 1), 0)
    ys = []
    for gi, w in enumerate(POOL_WINDOWS):
        acc = ext[:, gi * gd:(gi + 1) * gd]
        span = 1
        while span < w:
            acc = acc + pltpu.roll(acc, span, 0)
            span *= 2
        cnt = jnp.minimum(t + 1, w).astype(F32)
        pooled = acc[POOL_HALO:, :] / cnt - u[:, gi * gd:(gi + 1) * gd]
        ys.append(_bdot(pooled, wgrp_ref[gi]))
    y = jnp.concatenate(ys, axis=-1) * scale_ref[...]
    o_ref[...] = x + y


def _pool_layer(h, g, w_in, w_grp, scale, B, T):
    N, D = h.shape
    tm = TM_MIX
    nt = T // tm
    gd = D // len(POOL_WINDOWS)
    row = lambda b, i: (b * nt + i, 0)
    return pl.pallas_call(
        functools.partial(_pool_kernel, tm=tm, gd=gd),
        grid=(B, nt),
        in_specs=[
            pl.BlockSpec((tm, D), row),
            pl.BlockSpec((1, D), lambda b, i: (0, 0)),
            pl.BlockSpec((D, D), lambda b, i: (0, 0)),
            pl.BlockSpec((len(POOL_WINDOWS), gd, gd), lambda b, i: (0, 0, 0)),
            pl.BlockSpec((1, D), lambda b, i: (0, 0)),
        ],
        out_specs=pl.BlockSpec((tm, D), row),
        out_shape=jax.ShapeDtypeStruct((N, D), F32),
        scratch_shapes=[pltpu.VMEM((POOL_HALO, D), F32)],
        compiler_params=_cparams("parallel", "arbitrary"),
        name="pool_mixer",
    )(h, g.reshape(1, D), w_in.astype(BF16), w_grp.astype(BF16), scale.reshape(1, D))


def _gdn_in_kernel(h_ref, g_ref, wqkv_ref, wz_ref, wab_ref, wabt_ref, conv_ref, pcol_ref, prow_ref,
                   q_ref, k_ref, v_ref, z_ref, gbc_ref, gbr_ref, halo_ref, *, tm, hd, nh):
    i = pl.program_id(1)

    @pl.when(i == 0)
    def _():
        halo_ref[...] = jnp.zeros_like(halo_ref)

    xn = _rms(h_ref[...], g_ref[...]).astype(BF16)
    pq = jnp.dot(xn, wqkv_ref[...], preferred_element_type=F32)
    ext = jnp.concatenate([halo_ref[...], pq], axis=0)
    halo_ref[...] = pq[tm - CONV_HALO:, :]
    cw = conv_ref[...]
    conv = ext * cw[GDN_CONV - 1:GDN_CONV, :]
    for j in range(1, GDN_CONV):
        conv = conv + pltpu.roll(ext, j, 0) * cw[GDN_CONV - 1 - j:GDN_CONV - j, :]
    qkv = _silu(conv[CONV_HALO:, :])
    dh = hd // nh
    q = qkv[:, :hd]
    k = qkv[:, hd:2 * hd]
    q_ref[...] = (q * lax.rsqrt(_head_sumsq(q, nh, dh) + RMS_EPS)).astype(BF16)
    k_ref[...] = (k * lax.rsqrt(_head_sumsq(k, nh, dh) + RMS_EPS)).astype(BF16)
    v_ref[...] = qkv[:, 2 * hd:].astype(BF16)
    z_ref[...] = jnp.dot(xn, wz_ref[...], preferred_element_type=F32).astype(BF16)
    ab = jnp.dot(xn, wab_ref[...], preferred_element_type=F32)
    pc = pcol_ref[...]
    lane = lax.broadcasted_iota(I32, ab.shape, 1)
    gbc_ref[...] = jnp.where(lane < nh, -pc[0:1, :] * _softplus(ab + pc[1:2, :]), _sigmoid(ab))
    abt = lax.dot_general(wabt_ref[...], xn, NT_DIMS, preferred_element_type=F32)
    pr = prow_ref[...]
    srow = lax.broadcasted_iota(I32, abt.shape, 0)
    gbr_ref[...] = jnp.where(srow < nh, -pr[:, 0:1] * _softplus(abt + pr[:, 1:2]), _sigmoid(abt))


def _gdn_chunk_kernel(q_ref, k_ref, v_ref, gbc_ref, gbr_ref, o_ref, s_ref, *, tb, nh, dh):
    @pl.when(pl.program_id(1) == 0)
    def _():
        s_ref[...] = jnp.zeros_like(s_ref)

    C = GDN_CHUNK
    ri = lax.broadcasted_iota(I32, (C, C), 0)
    ci = lax.broadcasted_iota(I32, (C, C), 1)
    incl = ri >= ci
    strict = ri > ci
    tri = incl.astype(F32)
    trit = (ri <= ci).astype(F32)
    scale = dh ** -0.5
    nc = tb // C
    prep = {}
    state = [s_ref[hh] for hh in range(nh)]

    def prepare(c, hh):
        r0 = c * C
        sl = slice(hh * dh, (hh + 1) * dh)
        q = q_ref[r0:r0 + C, sl].astype(F32) * scale
        k = k_ref[r0:r0 + C, sl].astype(F32)
        v = v_ref[r0:r0 + C, sl].astype(F32)
        g_col = gbc_ref[r0:r0 + C, hh:hh + 1]
        beta = gbc_ref[r0:r0 + C, nh + hh:nh + hh + 1]
        g_row = gbr_ref[hh:hh + 1, r0:r0 + C]
        gc_col = jnp.sum(tri * g_row, axis=1, keepdims=True)
        gc_row = jnp.sum(trit * g_col, axis=0, keepdims=True)
        decay = jnp.where(incl, jnp.exp(jnp.where(incl, gc_col - gc_row, 0.0)), 0.0)
        kb = k * beta
        kq = _bdot_nt(jnp.concatenate([kb, q], axis=0), k)
        yield
        L = jnp.where(strict, kq[:C] * decay, 0.0)
        aqk = kq[C:] * decay
        eg = jnp.exp(gc_col)
        x = jnp.concatenate([v * beta, kb * eg], axis=1)
        lx = _bdot(L, x)
        p = _bdot(L, L)
        yield
        x = x - lx
        for it in range(5):
            px = _bdot(p, x)
            if it < 4:
                p2 = _bdot(p, p)
            yield
            x = x + px
            p = p2
        g_last = gc_col[C - 1:C, :]
        prep[(c, hh)] = dict(u=x[:, :dh], wq=jnp.concatenate([x[:, dh:], q * eg], axis=0), aqk=aqk,
                             k_tail=k * jnp.exp(g_last - gc_col), dec=jnp.exp(g_last))

    def recur(c, hh):
        r0 = c * C
        sl = slice(hh * dh, (hh + 1) * dh)
        a = prep.pop((c, hh))
        ws = _bdot(a["wq"], state[hh])
        yield
        v_new = a["u"] - ws[:C]
        ov = _bdot(a["aqk"], v_new)
        kv = lax.dot_general(a["k_tail"].astype(BF16), v_new.astype(BF16), TN_DIMS,
                             preferred_element_type=F32)
        yield
        o_ref[r0:r0 + C, sl] = (ws[C:] + ov).astype(BF16)
        state[hh] = state[hh] * a["dec"] + kv

    _round_robin([prepare(0, hh) for hh in range(nh)])
    for c in range(nc):
        nxt = [prepare(c + 1, hh) for hh in range(nh)] if c + 1 < nc else []
        _round_robin([recur(c, hh) for hh in range(nh)] + nxt)
    for hh in range(nh):
        s_ref[hh] = state[hh]


def _gdn_out_kernel(o_ref, z_ref, h_ref, ng_ref, wout_ref, out_ref, *, nh, dh):
    o = o_ref[...].astype(F32)
    on = o * lax.rsqrt(_head_sumsq(o, nh, dh) * (1.0 / dh) + RMS_EPS) * ng_ref[...]
    y = on * _silu(z_ref[...].astype(F32))
    out_ref[...] = h_ref[...] + _bdot(y, wout_ref[...])


def _gdn_layer(h, g, w_in, conv_w, a_log, dt_bias, norm_g, w_out, B, T):
    N, D = h.shape
    nh = GDN_HEADS
    hd = w_out.shape[0]
    dh = hd // nh
    tm = TM_MIX
    nt = T // tm
    w_qkv = w_in[:, :3 * hd].astype(BF16)
    w_z = w_in[:, 3 * hd:4 * hd].astype(BF16)
    w_ab = w_in[:, 4 * hd:]
    w_ab_pad = jnp.pad(w_ab, ((0, 0), (0, LANES - 2 * nh))).astype(BF16)
    w_ab_t = w_ab.T.astype(BF16)
    amul = jnp.exp(a_log.astype(F32))
    pcol = jnp.zeros((2, LANES), F32).at[0, :nh].set(amul).at[1, :nh].set(dt_bias)
    prow = jnp.zeros((2 * nh, 2), F32).at[:nh, 0].set(amul).at[:nh, 1].set(dt_bias)
    row = lambda b, i: (b * nt + i, 0)
    const2 = lambda b, i: (0, 0)
    q, k, v, z, gbc, gbr = pl.pallas_call(
        functools.partial(_gdn_in_kernel, tm=tm, hd=hd, nh=nh),
        grid=(B, nt),
        in_specs=[
            pl.BlockSpec((tm, D), row),
            pl.BlockSpec((1, D), const2),
            pl.BlockSpec((D, 3 * hd), const2),
            pl.BlockSpec((D, hd), const2),
            pl.BlockSpec((D, LANES), const2),
            pl.BlockSpec((2 * nh, D), const2),
            pl.BlockSpec((GDN_CONV, 3 * hd), const2),
            pl.BlockSpec((2, LANES), const2),
            pl.BlockSpec((2 * nh, 2), const2),
        ],
        out_specs=[
            pl.BlockSpec((tm, hd), row),
            pl.BlockSpec((tm, hd), row),
            pl.BlockSpec((tm, hd), row),
            pl.BlockSpec((tm, hd), row),
            pl.BlockSpec((tm, LANES), row),
            pl.BlockSpec((None, 2 * nh, tm), lambda b, i: (b, 0, i)),
        ],
        out_shape=[
            jax.ShapeDtypeStruct((N, hd), BF16),
            jax.ShapeDtypeStruct((N, hd), BF16),
            jax.ShapeDtypeStruct((N, hd), BF16),
            jax.ShapeDtypeStruct((N, hd), BF16),
            jax.ShapeDtypeStruct((N, LANES), F32),
            jax.ShapeDtypeStruct((B, 2 * nh, T), F32),
        ],
        scratch_shapes=[pltpu.VMEM((CONV_HALO, 3 * hd), F32)],
        compiler_params=_cparams("parallel", "arbitrary"),
        name="gdn_in",
    )(h, g.reshape(1, D), w_qkv, w_z, w_ab_pad, w_ab_t, conv_w.astype(F32), pcol, prow)

    tb = GDN_TB
    ntb = T // tb
    rowb = lambda b, i: (b * ntb + i, 0)
    o = pl.pallas_call(
        functools.partial(_gdn_chunk_kernel, tb=tb, nh=nh, dh=dh),
        grid=(B, ntb),
        in_specs=[
            pl.BlockSpec((tb, hd), rowb),
            pl.BlockSpec((tb, hd), rowb),
            pl.BlockSpec((tb, hd), rowb),
            pl.BlockSpec((tb, LANES), rowb),
            pl.BlockSpec((None, 2 * nh, tb), lambda b, i: (b, 0, i)),
        ],
        out_specs=pl.BlockSpec((tb, hd), rowb),
        out_shape=jax.ShapeDtypeStruct((N, hd), BF16),
        scratch_shapes=[pltpu.VMEM((nh, dh, dh), F32)],
        compiler_params=_cparams("parallel", "arbitrary"),
        name="gdn_delta_rule",
    )(q, k, v, gbc, gbr)

    row1 = lambda i: (i, 0)
    const1 = lambda i: (0, 0)
    to = TM_OUT
    return pl.pallas_call(
        functools.partial(_gdn_out_kernel, nh=nh, dh=dh),
        grid=(N // to,),
        in_specs=[
            pl.BlockSpec((to, hd), row1),
            pl.BlockSpec((to, hd), row1),
            pl.BlockSpec((to, D), row1),
            pl.BlockSpec((1, hd), const1),
            pl.BlockSpec((hd, D), const1),
        ],
        out_specs=pl.BlockSpec((to, D), row1),
        out_shape=jax.ShapeDtypeStruct((N, D), F32),
        compiler_params=_cparams("parallel"),
        name="gdn_out",
    )(o, z, h, jnp.tile(norm_g.astype(F32), nh).reshape(1, hd), w_out.astype(BF16))


def _fox_in_kernel(h_ref, g_ref, w_ref, wft_ref, bf_ref, qg_ref, kg_ref,
                   q_ref, k_ref, v_ref, og_ref, c_ref, carry_ref, *, tm, hd, nh):
    i = pl.program_id(1)

    @pl.when(i == 0)
    def _():
        carry_ref[...] = jnp.zeros_like(carry_ref)

    dh = hd // nh
    xn = _rms(h_ref[...], g_ref[...]).astype(BF16)
    pq = jnp.dot(xn, w_ref[...], preferred_element_type=F32)
    q = pq[:, :hd]
    k = pq[:, hd:2 * hd]
    q_ref[...] = (q * lax.rsqrt(_head_sumsq(q, nh, dh) * (1.0 / dh) + RMS_EPS)
                  * (qg_ref[...] * (dh ** -0.5 * LOG2E))).astype(BF16)
    k_ref[...] = (k * lax.rsqrt(_head_sumsq(k, nh, dh) * (1.0 / dh) + RMS_EPS) * kg_ref[...]).astype(BF16)
    v_ref[...] = pq[:, 2 * hd:3 * hd].astype(BF16)
    og_ref[...] = pq[:, 3 * hd:].astype(BF16)
    ft = lax.dot_general(wft_ref[...], xn, NT_DIMS, preferred_element_type=F32) + bf_ref[...]
    logf = jnp.minimum(ft, 0.0) - jnp.log(1.0 + jnp.exp(-jnp.abs(ft)))
    upper = (lax.broadcasted_iota(I32, (tm, tm), 0) <= lax.broadcasted_iota(I32, (tm, tm), 1)).astype(F32)
    c = jnp.dot(logf, upper, preferred_element_type=F32, precision=lax.Precision.HIGHEST) + carry_ref[...]
    c_ref[...] = c * LOG2E
    carry_ref[...] = c[:, tm - 1:tm]


def _fox_attn_kernel(q_ref, k_ref, v_ref, c_ref, o_ref, m_ref, l_ref, acc_ref, *, tq, sq, tk):
    ns = tq // sq
    q0 = pl.program_id(2) * tq
    m_ref[...] = jnp.full_like(m_ref, -jnp.inf)
    l_ref[...] = jnp.zeros_like(l_ref)
    acc_ref[...] = jnp.zeros_like(acc_ref)

    def stream(a, blocks):
        q = q_ref[a * sq:(a + 1) * sq, :]
        for start, size, tri in blocks:
            kb = k_ref[pl.ds(start, size), :]
            vb = v_ref[pl.ds(start, size), :]
            s = lax.dot_general(q, kb, NT_DIMS, preferred_element_type=F32) - c_ref[:, pl.ds(start, size)]
            yield
            if tri:
                s = jnp.where(lax.broadcasted_iota(I32, (sq, size), 0) >= lax.broadcasted_iota(I32, (sq, size), 1),
                              s, -jnp.inf)
            m_old = m_ref[a]
            m_new = jnp.maximum(m_old, jnp.max(s, axis=-1, keepdims=True))
            alpha = jnp.exp2(m_old - m_new)
            p = jnp.exp2(s - jnp.tile(m_new, (1, size // LANES)))
            l_new = l_ref[a] * alpha + jnp.sum(p, axis=-1, keepdims=True)
            pv = jnp.dot(p.astype(BF16), vb, preferred_element_type=F32)
            yield
            m_ref[a] = m_new
            l_ref[a] = l_new
            acc_ref[a] = acc_ref[a] * alpha + pv

    def body(j, carry):
        start = pl.multiple_of(j * tk, tk)
        _round_robin([stream(a, [(start, tk, False)]) for a in range(ns)])
        return carry

    lax.fori_loop(0, q0 // tk, body, 0)

    diag = []
    for a in range(ns):
        blocks, off = [], 0
        while off < a * sq:
            size = min(tk, a * sq - off)
            blocks.append((pl.multiple_of(q0 + off, sq), size, False))
            off += size
        blocks.append((pl.multiple_of(q0 + a * sq, sq), sq, True))
        diag.append(stream(a, blocks))
    _round_robin(diag)
    for a in range(ns):
        o_ref[a * sq:(a + 1) * sq, :] = (acc_ref[a] / l_ref[a]).astype(BF16)


def _fox_out_kernel(o_ref, og_ref, h_ref, wout_ref, out_ref):
    y = o_ref[...].astype(F32) * _sigmoid(og_ref[...].astype(F32))
    out_ref[...] = h_ref[...] + _bdot(y, wout_ref[...])


def _fox_layer(h, g, w_in, b_f, qn_g, kn_g, w_out, B, T):
    N, D = h.shape
    nh = FOX_HEADS
    hd = w_out.shape[0]
    dh = hd // nh
    tm = TM_MIX
    nt = T // tm
    row = lambda b, i: (b * nt + i, 0)
    const2 = lambda b, i: (0, 0)
    q, k, v, og, c = pl.pallas_call(
        functools.partial(_fox_in_kernel, tm=tm, hd=hd, nh=nh),
        grid=(B, nt),
        in_specs=[
            pl.BlockSpec((tm, D), row),
            pl.BlockSpec((1, D), const2),
            pl.BlockSpec((D, 4 * hd), const2),
            pl.BlockSpec((nh, D), const2),
            pl.BlockSpec((nh, 1), const2),
            pl.BlockSpec((1, hd), const2),
            pl.BlockSpec((1, hd), const2),
        ],
        out_specs=[
            pl.BlockSpec((tm, hd), row),
            pl.BlockSpec((tm, hd), row),
            pl.BlockSpec((tm, hd), row),
            pl.BlockSpec((tm, hd), row),
            pl.BlockSpec((None, nh, tm), lambda b, i: (b, 0, i)),
        ],
        out_shape=[
            jax.ShapeDtypeStruct((N, hd), BF16),
            jax.ShapeDtypeStruct((N, hd), BF16),
            jax.ShapeDtypeStruct((N, hd), BF16),
            jax.ShapeDtypeStruct((N, hd), BF16),
            jax.ShapeDtypeStruct((B, nh, T), F32),
        ],
        scratch_shapes=[pltpu.VMEM((nh, 1), F32)],
        compiler_params=_cparams("parallel", "arbitrary"),
        name="fox_in",
    )(h, g.reshape(1, D), w_in[:, :4 * hd].astype(BF16), w_in[:, 4 * hd:].T.astype(BF16),
      b_f.astype(F32).reshape(nh, 1), jnp.tile(qn_g.astype(F32), nh).reshape(1, hd),
      jnp.tile(kn_g.astype(F32), nh).reshape(1, hd))

    tq, sq, tk = FOX_TQ, FOX_SQ, FOX_TK
    assert dh == LANES and T % tq == 0 and tq % sq == 0 and tq % tk == 0
    nq = T // tq
    o = pl.pallas_call(
        functools.partial(_fox_attn_kernel, tq=tq, sq=sq, tk=tk),
        grid=(B, nh, nq),
        in_specs=[
            pl.BlockSpec((tq, dh), lambda b, hh, i: (b * nq + i, hh)),
            pl.BlockSpec((T, dh), lambda b, hh, i: (b, hh)),
            pl.BlockSpec((T, dh), lambda b, hh, i: (b, hh)),
            pl.BlockSpec((None, 1, T), lambda b, hh, i: (b * nh + hh, 0, 0)),
        ],
        out_specs=pl.BlockSpec((tq, dh), lambda b, hh, i: (b * nq + i, hh)),
        out_shape=jax.ShapeDtypeStruct((N, hd), BF16),
        scratch_shapes=[pltpu.VMEM((tq // sq, sq, LANES), F32)] * 3,
        compiler_params=_cparams("parallel", "parallel", "arbitrary"),
        name="fox_attention",
    )(q, k, v, c.reshape(B * nh, 1, T))

    row1 = lambda i: (i, 0)
    const1 = lambda i: (0, 0)
    to = TM_OUT
    return pl.pallas_call(
        _fox_out_kernel,
        grid=(N // to,),
        in_specs=[
            pl.BlockSpec((to, hd), row1),
            pl.BlockSpec((to, hd), row1),
            pl.BlockSpec((to, D), row1),
            pl.BlockSpec((hd, D), const1),
        ],
        out_specs=pl.BlockSpec((to, D), row1),
        out_shape=jax.ShapeDtypeStruct((N, D), F32),
        compiler_params=_cparams("parallel"),
        name="fox_out",
    )(o, og, h, w_out.astype(BF16))


def _router_kernel(h_ref, g_ref, wt_ref, bt_ref, xs_ref, meta_ref, cnt_ref, run_ref, *, tm, d):
    i = pl.program_id(0)

    @pl.when(i == 0)
    def _():
        run_ref[...] = jnp.zeros_like(run_ref)

    xn = _rms(h_ref[...], g_ref[...])
    x_hi = xn.astype(BF16)
    x_hi32 = x_hi.astype(F32)
    x_lo = (xn - x_hi32).astype(BF16)
    nt = lambda a, b: lax.dot_general(a, b, NT_DIMS, preferred_element_type=F32)
    logits = (nt(wt_ref[0], x_hi) + nt(wt_ref[0], x_lo) + nt(wt_ref[1], x_hi)) + bt_ref[...]
    col = lambda j: logits[j:j + 1, :]
    gl = [col(j) for j in range(MOE_GROUPS)]
    gmax = functools.reduce(jnp.maximum, gl)
    gidx = jnp.full_like(gmax, MOE_GROUPS - 1).astype(I32)
    for j in range(MOE_GROUPS - 2, -1, -1):
        gidx = jnp.where(gl[j] == gmax, j, gidx)
    grp_p = 1.0 / functools.reduce(jnp.add, [jnp.exp(l - gmax) for l in gl])
    ev = []
    for kk in range(MOE_PER_GROUP):
        e = col(MOE_GROUPS + (MOE_GROUPS - 1) * MOE_PER_GROUP + kk)
        for gg in range(MOE_GROUPS - 2, -1, -1):
            e = jnp.where(gidx == gg, col(MOE_GROUPS + gg * MOE_PER_GROUP + kk), e)
        ev.append(e)
    m1 = functools.reduce(jnp.maximum, ev)
    i1 = jnp.full_like(gidx, MOE_PER_GROUP - 1)
    for kk in range(MOE_PER_GROUP - 2, -1, -1):
        i1 = jnp.where(ev[kk] == m1, kk, i1)
    ev2 = [jnp.where(i1 == kk, -jnp.inf, ev[kk]) for kk in range(MOE_PER_GROUP)]
    m2 = functools.reduce(jnp.maximum, ev2)
    i2 = jnp.full_like(gidx, MOE_PER_GROUP - 1)
    for kk in range(MOE_PER_GROUP - 2, -1, -1):
        i2 = jnp.where((ev2[kk] == m2) & (i1 != kk), kk, i2)
    i2 = jnp.where((i1 == MOE_PER_GROUP - 1) & (i2 == MOE_PER_GROUP - 1), MOE_PER_GROUP - 2, i2)
    e2 = jnp.exp(m2 - m1)
    w1 = grp_p / (1.0 + e2)
    w2 = grp_p * e2 / (1.0 + e2)
    lo = jnp.minimum(i1, i2)
    hi = jnp.maximum(i1, i2)
    w_lo = jnp.where(i1 < i2, w1, w2)
    w_hi = jnp.where(i1 < i2, w2, w1)
    base = jnp.where(lo == 0, 0, jnp.where(lo == 1, 3, 5))
    cls = gidx * MOE_PAIRS + base + hi - lo - 1
    srow = lax.broadcasted_iota(I32, (CLASS_ROWS, tm), 0)
    onehot = (srow == cls).astype(F32)
    earlier = (lax.broadcasted_iota(I32, (tm, tm), 0) < lax.broadcasted_iota(I32, (tm, tm), 1))
    before = jnp.dot(onehot.astype(BF16), earlier.astype(BF16), preferred_element_type=F32)
    run = run_ref[...]
    rank = jnp.sum(onehot * (before + run), axis=0, keepdims=True)
    run = run + jnp.sum(onehot, axis=1, keepdims=True)
    run_ref[...] = run
    cnt_ref[...] = jnp.broadcast_to(run, cnt_ref.shape)
    zero = jnp.zeros_like(w_lo)
    meta_ref[...] = jnp.concatenate([w_lo, w_hi, cls.astype(F32), rank, zero, zero, zero, zero], axis=0)
    wrows = jnp.concatenate([w_lo, w_hi, jnp.zeros((LANES - 2, tm), F32)], axis=0)
    wbits = pltpu.bitcast(jnp.transpose(wrows), jnp.uint32)
    xs_ref[...] = jnp.concatenate([_pack_rounded_pairs(x_hi32), wbits], axis=-1)


def _pack_rounded_pairs(xr):
    half = xr.shape[1] // 2
    bits = pltpu.bitcast(xr, jnp.uint32)
    return lax.shift_right_logical(bits[:, :half], jnp.uint32(16)) | bits[:, half:]


def _pack_bf16_pairs(x):
    return _pack_rounded_pairs(x.astype(BF16).astype(F32))


def _unpack_bf16_pairs(word):
    lo = pltpu.bitcast(lax.shift_left(word, jnp.uint32(16)), F32)
    hi = pltpu.bitcast(word & jnp.uint32(0xFFFF0000), F32)
    return jnp.concatenate([lo, hi], axis=-1)


def _unpack_rows(x, half):
    gates = pltpu.bitcast(x[:, half:], F32)
    return _unpack_bf16_pairs(x[:, :half]).astype(BF16), gates[:, 0:1], gates[:, 1:2]


def _position_kernel(meta_ref, off_ref, pos_ref):
    cls = meta_ref[2:3, :].astype(I32)
    srow = lax.broadcasted_iota(I32, (CLASS_ROWS, cls.shape[1]), 0)
    off = jnp.sum(jnp.where(srow == cls, off_ref[...], 0.0), axis=0, keepdims=True)
    pos_ref[...] = (off + meta_ref[3:4, :]).astype(I32)


def _dispatch_kernel(pos_ref, xs_ref, out_ref, sem, *, tm):
    copies = []
    for r in range(tm):
        pos = pos_ref[0, 0, r]
        cp = pltpu.make_async_copy(xs_ref.at[pl.ds(r, 1), :], out_ref.at[pl.ds(pos, 1), :], sem)
        cp.start(priority=r % 2)
        copies.append(cp)
    for cp in copies:
        cp.wait()


def _expert_kernel(ea_ref, eb_ref, nv_ref, x_ref, wgua_ref, wda_ref, wgub_ref, wdb_ref, y_ref, *, d, dff):
    j = pl.program_id(0)

    @pl.when(j < nv_ref[0])
    def _():
        xn, wa, wb = _unpack_rows(x_ref[...], d // 2)

        def expert(wgu_ref, wd_ref):
            hgu = jnp.dot(xn, wgu_ref[...], preferred_element_type=F32)
            act = _silu(hgu[:, :dff]) * hgu[:, dff:]
            return jnp.dot(act.astype(BF16), wd_ref[...], preferred_element_type=F32)

        y_ref[...] = _pack_bf16_pairs(wa * expert(wgua_ref, wda_ref) + wb * expert(wgub_ref, wdb_ref))


def _combine_kernel(pos_ref, h_ref, ys_ref, out_ref, buf_ref, sem, *, tm):
    copies = []
    for r in range(tm):
        pos = pos_ref[0, 0, r]
        cp = pltpu.make_async_copy(ys_ref.at[pl.ds(pos, 1), :], buf_ref.at[pl.ds(r, 1), :], sem)
        cp.start(priority=r % 2)
        copies.append(cp)
    for cp in copies:
        cp.wait()
    out_ref[...] = h_ref[...] + _unpack_bf16_pairs(buf_ref[...])


def _moe_layer(h, g, w_group, b_group, w_router, b_router, w_gate, w_up, w_down):
    N, D = h.shape
    ne, _, dff = w_gate.shape
    tm = TM_ROUTE
    nb = N // tm
    n_log = MOE_GROUPS + ne
    assert n_log <= CLASS_ROWS
    w_rt = jnp.pad(jnp.concatenate([w_group, w_router], axis=1).astype(F32).T, ((0, CLASS_ROWS - n_log), (0, 0)))
    w_rt_hi = w_rt.astype(BF16)
    w_rt = jnp.stack([w_rt_hi, (w_rt - w_rt_hi.astype(F32)).astype(BF16)])
    b_rt = jnp.pad(jnp.concatenate([b_group, b_router]).astype(F32), (0, CLASS_ROWS - n_log)).reshape(CLASS_ROWS, 1)
    DX = D // 2 + LANES
    row1 = lambda i: (i, 0)
    const1 = lambda i: (0, 0)
    xs, meta, cnt = pl.pallas_call(
        functools.partial(_router_kernel, tm=tm, d=D),
        grid=(nb,),
        in_specs=[
            pl.BlockSpec((tm, D), row1),
            pl.BlockSpec((1, D), const1),
            pl.BlockSpec((2, CLASS_ROWS, D), lambda i: (0, 0, 0)),
            pl.BlockSpec((CLASS_ROWS, 1), const1),
        ],
        out_specs=[
            pl.BlockSpec((tm, DX), row1),
            pl.BlockSpec((8, tm), lambda i: (0, i)),
            pl.BlockSpec((CLASS_ROWS, LANES), const1),
        ],
        out_shape=[
            jax.ShapeDtypeStruct((N, DX), jnp.uint32),
            jax.ShapeDtypeStruct((8, N), F32),
            jax.ShapeDtypeStruct((CLASS_ROWS, LANES), F32),
        ],
        scratch_shapes=[pltpu.VMEM((CLASS_ROWS, 1), F32)],
        compiler_params=_cparams("arbitrary"),
        name="moe_router",
    )(h, g.reshape(1, D), w_rt, b_rt)

    counts = cnt[:MOE_CLASSES, 0].astype(I32)
    tiles_per = (counts + TE - 1) // TE
    tile_end = jnp.cumsum(tiles_per)
    offsets = jnp.pad((tile_end - tiles_per) * TE, (0, CLASS_ROWS - MOE_CLASSES))
    n_tiles = N // TE + MOE_CLASSES
    n_valid = tile_end[-1]
    tile_id = jnp.minimum(jnp.arange(n_tiles, dtype=I32), n_valid - 1)
    tile_cls = jnp.sum((tile_end[None, :] <= tile_id[:, None]).astype(I32), axis=1)
    grp = tile_cls // MOE_PAIRS
    pid = tile_cls % MOE_PAIRS
    tile_ea = (grp * MOE_PER_GROUP + jnp.asarray(PAIR_LO, I32)[pid]).astype(I32)
    tile_eb = (grp * MOE_PER_GROUP + jnp.asarray(PAIR_HI, I32)[pid]).astype(I32)
    tp = TM_PERMUTE
    npb = N // tp
    tpos = min(8192, N)
    pos3 = pl.pallas_call(
        _position_kernel,
        grid=(N // tpos,),
        in_specs=[pl.BlockSpec((8, tpos), lambda i: (0, i)), pl.BlockSpec((CLASS_ROWS, 1), const1)],
        out_specs=pl.BlockSpec((1, tpos), lambda i: (0, i)),
        out_shape=jax.ShapeDtypeStruct((1, N), I32),
        compiler_params=_cparams("parallel"),
        name="moe_positions",
    )(meta, offsets.astype(F32).reshape(CLASS_ROWS, 1)).reshape(npb, 1, tp)

    P = n_tiles * TE
    smem3 = pl.BlockSpec((1, 1, tp), lambda i: (i, 0, 0), memory_space=pltpu.SMEM)
    xsort = pl.pallas_call(
        functools.partial(_dispatch_kernel, tm=tp),
        grid=(npb,),
        in_specs=[smem3, pl.BlockSpec((tp, DX), row1)],
        out_specs=pl.BlockSpec(memory_space=pl.ANY),
        out_shape=jax.ShapeDtypeStruct((P, DX), jnp.uint32),
        scratch_shapes=[pltpu.SemaphoreType.DMA(())],
        compiler_params=_cparams("arbitrary"),
        name="moe_dispatch",
    )(pos3, xs)

    w_gu = jnp.concatenate([w_gate, w_up], axis=-1).astype(BF16)
    w_dn = w_down.astype(BF16)
    ysort = pl.pallas_call(
        functools.partial(_expert_kernel, d=D, dff=dff),
        grid_spec=pltpu.PrefetchScalarGridSpec(
            num_scalar_prefetch=3,
            grid=(n_tiles,),
            in_specs=[
                pl.BlockSpec((TE, DX), lambda j, ea, eb, nv: (j, 0)),
                pl.BlockSpec((None, D, 2 * dff), lambda j, ea, eb, nv: (ea[j], 0, 0)),
                pl.BlockSpec((None, dff, D), lambda j, ea, eb, nv: (ea[j], 0, 0)),
                pl.BlockSpec((None, D, 2 * dff), lambda j, ea, eb, nv: (eb[j], 0, 0)),
                pl.BlockSpec((None, dff, D), lambda j, ea, eb, nv: (eb[j], 0, 0)),
            ],
            out_specs=pl.BlockSpec((TE, D // 2), lambda j, ea, eb, nv: (j, 0)),
        ),
        out_shape=jax.ShapeDtypeStruct((P, D // 2), jnp.uint32),
        compiler_params=_cparams("arbitrary"),
        name="moe_experts",
    )(tile_ea, tile_eb, n_valid.reshape(1).astype(I32), xsort, w_gu, w_dn, w_gu, w_dn)

    return pl.pallas_call(
        functools.partial(_combine_kernel, tm=tp),
        grid=(npb,),
        in_specs=[smem3, pl.BlockSpec((tp, D), row1), pl.BlockSpec(memory_space=pl.ANY)],
        out_specs=pl.BlockSpec((tp, D), row1),
        out_shape=jax.ShapeDtypeStruct((N, D), F32),
        scratch_shapes=[pltpu.VMEM((tp, D // 2), jnp.uint32), pltpu.SemaphoreType.DMA(())],
        compiler_params=_cparams("arbitrary"),
        name="moe_combine",
    )(pos3, h, ysort)


def kernel(x, norm_mix_g, norm_ffn_g, pool_w_in, pool_w_grp, pool_scale, gdn_w_in, gdn_conv_w, gdn_a_log, gdn_dt_bias, gdn_norm_g, gdn_w_out, fox_w_in, fox_b_f, fox_q_norm_g, fox_k_norm_g, fox_w_out, moe_w_group, moe_b_group, moe_w_router, moe_b_router, moe_w_gate, moe_w_up, moe_w_down):
    B, T, D = x.shape
    depth = norm_mix_g.shape[0]
    h = x.reshape(B * T, D)
    for i in range(depth):
        m, slot = i % 3, i // 3
        if m == 0:
            h = _pool_layer(h, norm_mix_g[i], pool_w_in[slot], pool_w_grp[slot], pool_scale[slot], B, T)
        elif m == 1:
            h = _gdn_layer(h, norm_mix_g[i], gdn_w_in[slot], gdn_conv_w[slot], gdn_a_log[slot],
                           gdn_dt_bias[slot], gdn_norm_g[slot], gdn_w_out[slot], B, T)
        else:
            h = _fox_layer(h, norm_mix_g[i], fox_w_in[slot], fox_b_f[slot], fox_q_norm_g[slot],
                           fox_k_norm_g[slot], fox_w_out[slot], B, T)
        h = _moe_layer(h, norm_ffn_g[i], moe_w_group[i], moe_b_group[i], moe_w_router[i], moe_b_router[i],
                       moe_w_gate[i], moe_w_up[i], moe_w_down[i])
    return h.reshape(B, T, D)
```

```python
import functools

import jax
import jax.numpy as jnp
from jax import lax
from jax.experimental import pallas as pl
from jax.experimental.pallas import tpu as pltpu

F32 = jnp.float32
BF16 = jnp.bfloat16
I32 = jnp.int32

RMS_EPS = 1e-6
LANES = 128
VMEM_LIMIT = 56 * 1024 * 1024

POOL_WINDOWS = (2, 4, 8, 16)
POOL_HALO = 16
GDN_HEADS = 8
GDN_CONV = 4
GDN_CHUNK = 64
CONV_HALO = 8
FOX_HEADS = 8
MOE_GROUPS = 4
MOE_PER_GROUP = 4
MOE_PAIRS = 6
MOE_CLASSES = MOE_GROUPS * MOE_PAIRS
CLASS_ROWS = 32
PAIR_LO = (0, 0, 0, 1, 1, 2)
PAIR_HI = (1, 2, 3, 2, 3, 3)

TM_MIX = 256
TM_FOX_IN = 512
TM_OUT = 512
TM_ROUTE = 512
TM_PERMUTE = 1024
TE = 512
GDN_TB = 512
FOX_TQ = 1024
FOX_SQ = 512
FOX_TK = 1024
LOG2E = 1.4426950408889634

NT_DIMS = (((1,), (1,)), ((), ()))
TN_DIMS = (((0,), (0,)), ((), ()))


def _cparams(*sem):
    return pltpu.CompilerParams(dimension_semantics=sem, vmem_limit_bytes=VMEM_LIMIT)


def _rms(x, g):
    return x * lax.rsqrt(jnp.mean(x * x, axis=-1, keepdims=True) + RMS_EPS) * g


def _sigmoid(x):
    return 1.0 / (1.0 + jnp.exp(-x))


def _silu(x):
    return x * _sigmoid(x)


def _softplus(x):
    return jnp.maximum(x, 0.0) + jnp.log(1.0 + jnp.exp(-jnp.abs(x)))


def _bdot(a, b):
    return jnp.dot(a.astype(BF16), b.astype(BF16), preferred_element_type=F32)


def _bdot_nt(a, b):
    return lax.dot_general(a.astype(BF16), b.astype(BF16), NT_DIMS, preferred_element_type=F32)


def _round_robin(gens):
    alive = list(gens)
    while alive:
        for gen in list(alive):
            try:
                next(gen)
            except StopIteration:
                alive.remove(gen)


def _head_sumsq(x, n_heads, dh):
    parts = []
    for hh in range(n_heads):
        blk = x[:, hh * dh:(hh + 1) * dh]
        s = jnp.sum(blk * blk, axis=-1, keepdims=True)
        parts.append(jnp.broadcast_to(s, blk.shape))
    return jnp.concatenate(parts, axis=-1)


def _pool_kernel(h_ref, g_ref, win_ref, wgrp_ref, scale_ref, band_ref, o_ref, halo_ref, *, tm, gd):
    i = pl.program_id(1)

    @pl.when(i == 0)
    def _():
        halo_ref[...] = jnp.zeros_like(halo_ref)

    x = h_ref[...]
    xn = _rms(x, g_ref[...])
    u = _bdot(xn, win_ref[...])
    ub = u.astype(BF16)
    ext = jnp.concatenate([halo_ref[...], ub], axis=0)
    halo_ref[...] = ub[tm - POOL_HALO:, :]
    t = i * tm + lax.broadcasted_iota(I32, (tm, 1), 0)
    sums = [jnp.dot(band_ref[gi], ext[:, gi * gd:(gi + 1) * gd], preferred_element_type=F32)
            for gi in range(len(POOL_WINDOWS))]
    pooled = [sums[gi] / jnp.minimum(t + 1, w).astype(F32) - u[:, gi * gd:(gi + 1) * gd]
              for gi, w in enumerate(POOL_WINDOWS)]
    ys = [_bdot(pooled[gi], wgrp_ref[gi]) for gi in range(len(POOL_WINDOWS))]
    y = jnp.concatenate(ys, axis=-1) * scale_ref[...]
    o_ref[...] = x + y


def _pool_layer(h, g, w_in, w_grp, scale, B, T):
    N, D = h.shape
    tm = TM_MIX
    nt = T // tm
    gd = D // len(POOL_WINDOWS)
    row = lambda b, i: (b * nt + i, 0)
    rt = jnp.arange(tm)[:, None] + POOL_HALO
    rr = jnp.arange(tm + POOL_HALO)[None, :]
    band = jnp.stack([((rr <= rt) & (rr > rt - w)).astype(BF16) for w in POOL_WINDOWS])
    return pl.pallas_call(
        functools.partial(_pool_kernel, tm=tm, gd=gd),
        grid=(B, nt),
        in_specs=[
            pl.BlockSpec((tm, D), row),
            pl.BlockSpec((1, D), lambda b, i: (0, 0)),
            pl.BlockSpec((D, D), lambda b, i: (0, 0)),
            pl.BlockSpec((len(POOL_WINDOWS), gd, gd), lambda b, i: (0, 0, 0)),
            pl.BlockSpec((1, D), lambda b, i: (0, 0)),
            pl.BlockSpec((len(POOL_WINDOWS), tm, tm + POOL_HALO), lambda b, i: (0, 0, 0)),
        ],
        out_specs=pl.BlockSpec((tm, D), row),
        out_shape=jax.ShapeDtypeStruct((N, D), F32),
        scratch_shapes=[pltpu.VMEM((POOL_HALO, D), BF16)],
        compiler_params=_cparams("parallel", "arbitrary"),
        name="pool_mixer",
    )(h, g.reshape(1, D), w_in.astype(BF16), w_grp.astype(BF16), scale.reshape(1, D), band)


def _gdn_in_kernel(h_ref, g_ref, wqkv_ref, wz_ref, wab_ref, wabt_ref, conv_ref, pcol_ref, prow_ref,
                   q_ref, k_ref, v_ref, z_ref, gbc_ref, gbr_ref, halo_ref, *, tm, hd, nh):
    i = pl.program_id(1)

    @pl.when(i == 0)
    def _():
        halo_ref[...] = jnp.zeros_like(halo_ref)

    xn = _rms(h_ref[...], g_ref[...]).astype(BF16)
    pq = jnp.dot(xn, wqkv_ref[...], preferred_element_type=F32)
    ext = jnp.concatenate([halo_ref[...], pq], axis=0)
    halo_ref[...] = pq[tm - CONV_HALO:, :]
    cw = conv_ref[...]
    conv = ext * cw[GDN_CONV - 1:GDN_CONV, :]
    for j in range(1, GDN_CONV):
        conv = conv + pltpu.roll(ext, j, 0) * cw[GDN_CONV - 1 - j:GDN_CONV - j, :]
    qkv = _silu(conv[CONV_HALO:, :])
    dh = hd // nh
    q = qkv[:, :hd]
    k = qkv[:, hd:2 * hd]
    q_ref[...] = (q * lax.rsqrt(_head_sumsq(q, nh, dh) + RMS_EPS)).astype(BF16)
    k_ref[...] = (k * lax.rsqrt(_head_sumsq(k, nh, dh) + RMS_EPS)).astype(BF16)
    v_ref[...] = qkv[:, 2 * hd:].astype(BF16)
    z_ref[...] = jnp.dot(xn, wz_ref[...], preferred_element_type=F32).astype(BF16)
    ab = jnp.dot(xn, wab_ref[...], preferred_element_type=F32)
    pc = pcol_ref[...]
    lane = lax.broadcasted_iota(I32, ab.shape, 1)
    gbc_ref[...] = jnp.where(lane < nh, -pc[0:1, :] * _softplus(ab + pc[1:2, :]), _sigmoid(ab))
    abt = lax.dot_general(wabt_ref[...], xn, NT_DIMS, preferred_element_type=F32)
    pr = prow_ref[...]
    srow = lax.broadcasted_iota(I32, abt.shape, 0)
    gbr_ref[...] = jnp.where(srow < nh, -pr[:, 0:1] * _softplus(abt + pr[:, 1:2]), _sigmoid(abt))


def _gdn_chunk_kernel(q_ref, k_ref, v_ref, gbc_ref, gbr_ref, o_ref, s_ref, *, tb, nh, dh):
    @pl.when(pl.program_id(1) == 0)
    def _():
        s_ref[...] = jnp.zeros_like(s_ref)

    C = GDN_CHUNK
    ri = lax.broadcasted_iota(I32, (C, C), 0)
    ci = lax.broadcasted_iota(I32, (C, C), 1)
    incl = ri >= ci
    strict = ri > ci
    tri = incl.astype(F32)
    trit = (ri <= ci).astype(F32)
    scale = dh ** -0.5
    nc = tb // C
    prep = {}
    state = [s_ref[hh] for hh in range(nh)]

    def prepare(c, hh):
        r0 = c * C
        sl = slice(hh * dh, (hh + 1) * dh)
        q = q_ref[r0:r0 + C, sl].astype(F32) * scale
        k = k_ref[r0:r0 + C, sl].astype(F32)
        v = v_ref[r0:r0 + C, sl].astype(F32)
        g_col = gbc_ref[r0:r0 + C, hh:hh + 1]
        beta = gbc_ref[r0:r0 + C, nh + hh:nh + hh + 1]
        g_row = gbr_ref[hh:hh + 1, r0:r0 + C]
        gc_col = jnp.sum(tri * g_row, axis=1, keepdims=True)
        gc_row = jnp.sum(trit * g_col, axis=0, keepdims=True)
        decay = jnp.where(incl, jnp.exp(jnp.where(incl, gc_col - gc_row, 0.0)), 0.0)
        kb = k * beta
        kq = _bdot_nt(jnp.concatenate([kb, q], axis=0), k)
        yield
        L = jnp.where(strict, kq[:C] * decay, 0.0)
        aqk = kq[C:] * decay
        eg = jnp.exp(gc_col)
        x = jnp.concatenate([v * beta, kb * eg], axis=1)
        lx = _bdot(L, x)
        p = _bdot(L, L)
        yield
        x = x - lx
        for it in range(5):
            px = _bdot(p, x)
            if it < 4:
                p2 = _bdot(p, p)
            yield
            x = x + px
            p = p2
        g_last = gc_col[C - 1:C, :]
        prep[(c, hh)] = dict(u=x[:, :dh], wq=jnp.concatenate([x[:, dh:], q * eg], axis=0), aqk=aqk,
                             k_tail=k * jnp.exp(g_last - gc_col), dec=jnp.exp(g_last))

    def recur(c, hh):
        r0 = c * C
        sl = slice(hh * dh, (hh + 1) * dh)
        a = prep.pop((c, hh))
        ws = _bdot(a["wq"], state[hh])
        yield
        v_new = a["u"] - ws[:C]
        ov = _bdot(a["aqk"], v_new)
        kv = lax.dot_general(a["k_tail"].astype(BF16), v_new.astype(BF16), TN_DIMS,
                             preferred_element_type=F32)
        yield
        o_ref[r0:r0 + C, sl] = (ws[C:] + ov).astype(BF16)
        state[hh] = state[hh] * a["dec"] + kv

    _round_robin([prepare(0, hh) for hh in range(nh)])
    for c in range(nc):
        nxt = [prepare(c + 1, hh) for hh in range(nh)] if c + 1 < nc else []
        _round_robin([recur(c, hh) for hh in range(nh)] + nxt)
    for hh in range(nh):
        s_ref[hh] = state[hh]


def _gdn_out_kernel(o_ref, z_ref, h_ref, ng_ref, wout_ref, out_ref, *, nh, dh):
    o = o_ref[...].astype(F32)
    on = o * lax.rsqrt(_head_sumsq(o, nh, dh) * (1.0 / dh) + RMS_EPS) * ng_ref[...]
    y = on * _silu(z_ref[...].astype(F32))
    out_ref[...] = h_ref[...] + _bdot(y, wout_ref[...])


def _gdn_layer(h, g, w_in, conv_w, a_log, dt_bias, norm_g, w_out, B, T):
    N, D = h.shape
    nh = GDN_HEADS
    hd = w_out.shape[0]
    dh = hd // nh
    tm = TM_MIX
    nt = T // tm
    w_qkv = w_in[:, :3 * hd].astype(BF16)
    w_z = w_in[:, 3 * hd:4 * hd].astype(BF16)
    w_ab = w_in[:, 4 * hd:]
    w_ab_pad = jnp.pad(w_ab, ((0, 0), (0, LANES - 2 * nh))).astype(BF16)
    w_ab_t = w_ab.T.astype(BF16)
    amul = jnp.exp(a_log.astype(F32))
    pcol = jnp.zeros((2, LANES), F32).at[0, :nh].set(amul).at[1, :nh].set(dt_bias)
    prow = jnp.zeros((2 * nh, 2), F32).at[:nh, 0].set(amul).at[:nh, 1].set(dt_bias)
    row = lambda b, i: (b * nt + i, 0)
    const2 = lambda b, i: (0, 0)
    q, k, v, z, gbc, gbr = pl.pallas_call(
        functools.partial(_gdn_in_kernel, tm=tm, hd=hd, nh=nh),
        grid=(B, nt),
        in_specs=[
            pl.BlockSpec((tm, D), row),
            pl.BlockSpec((1, D), const2),
            pl.BlockSpec((D, 3 * hd), const2),
            pl.BlockSpec((D, hd), const2),
            pl.BlockSpec((D, LANES), const2),
            pl.BlockSpec((2 * nh, D), const2),
            pl.BlockSpec((GDN_CONV, 3 * hd), const2),
            pl.BlockSpec((2, LANES), const2),
            pl.BlockSpec((2 * nh, 2), const2),
        ],
        out_specs=[
            pl.BlockSpec((tm, hd), row),
            pl.BlockSpec((tm, hd), row),
            pl.BlockSpec((tm, hd), row),
            pl.BlockSpec((tm, hd), row),
            pl.BlockSpec((tm, LANES), row),
            pl.BlockSpec((None, 2 * nh, tm), lambda b, i: (b, 0, i)),
        ],
        out_shape=[
            jax.ShapeDtypeStruct((N, hd), BF16),
            jax.ShapeDtypeStruct((N, hd), BF16),
            jax.ShapeDtypeStruct((N, hd), BF16),
            jax.ShapeDtypeStruct((N, hd), BF16),
            jax.ShapeDtypeStruct((N, LANES), F32),
            jax.ShapeDtypeStruct((B, 2 * nh, T), F32),
        ],
        scratch_shapes=[pltpu.VMEM((CONV_HALO, 3 * hd), F32)],
        compiler_params=_cparams("parallel", "arbitrary"),
        name="gdn_in",
    )(h, g.reshape(1, D), w_qkv, w_z, w_ab_pad, w_ab_t, conv_w.astype(F32), pcol, prow)

    tb = GDN_TB
    ntb = T // tb
    rowb = lambda b, i: (b * ntb + i, 0)
    o = pl.pallas_call(
        functools.partial(_gdn_chunk_kernel, tb=tb, nh=nh, dh=dh),
        grid=(B, ntb),
        in_specs=[
            pl.BlockSpec((tb, hd), rowb),
            pl.BlockSpec((tb, hd), rowb),
            pl.BlockSpec((tb, hd), rowb),
            pl.BlockSpec((tb, LANES), rowb),
            pl.BlockSpec((None, 2 * nh, tb), lambda b, i: (b, 0, i)),
        ],
        out_specs=pl.BlockSpec((tb, hd), rowb),
        out_shape=jax.ShapeDtypeStruct((N, hd), BF16),
        scratch_shapes=[pltpu.VMEM((nh, dh, dh), F32)],
        compiler_params=_cparams("parallel", "arbitrary"),
        name="gdn_delta_rule",
    )(q, k, v, gbc, gbr)

    row1 = lambda i: (i, 0)
    const1 = lambda i: (0, 0)
    to = TM_OUT
    return pl.pallas_call(
        functools.partial(_gdn_out_kernel, nh=nh, dh=dh),
        grid=(N // to,),
        in_specs=[
            pl.BlockSpec((to, hd), row1),
            pl.BlockSpec((to, hd), row1),
            pl.BlockSpec((to, D), row1),
            pl.BlockSpec((1, hd), const1),
            pl.BlockSpec((hd, D), const1),
        ],
        out_specs=pl.BlockSpec((to, D), row1),
        out_shape=jax.ShapeDtypeStruct((N, D), F32),
        compiler_params=_cparams("parallel"),
        name="gdn_out",
    )(o, z, h, jnp.tile(norm_g.astype(F32), nh).reshape(1, hd), w_out.astype(BF16))


def _fox_in_kernel(h_ref, g_ref, w_ref, wft_ref, bf_ref, qg_ref, kg_ref,
                   q_ref, k_ref, v_ref, og_ref, c_ref, carry_ref, *, tm, hd, nh):
    i = pl.program_id(1)

    @pl.when(i == 0)
    def _():
        carry_ref[...] = jnp.zeros_like(carry_ref)

    dh = hd // nh
    xn = _rms(h_ref[...], g_ref[...]).astype(BF16)
    pq = jnp.dot(xn, w_ref[...], preferred_element_type=F32)
    q = pq[:, :hd]
    k = pq[:, hd:2 * hd]
    q_ref[...] = (q * lax.rsqrt(_head_sumsq(q, nh, dh) * (1.0 / dh) + RMS_EPS)
                  * (qg_ref[...] * (dh ** -0.5 * LOG2E))).astype(BF16)
    k_ref[...] = (k * lax.rsqrt(_head_sumsq(k, nh, dh) * (1.0 / dh) + RMS_EPS) * kg_ref[...]).astype(BF16)
    v_ref[...] = pq[:, 2 * hd:3 * hd].astype(BF16)
    og_ref[...] = pq[:, 3 * hd:].astype(BF16)
    ft = lax.dot_general(wft_ref[...], xn, NT_DIMS, preferred_element_type=F32) + bf_ref[...]
    logf = jnp.minimum(ft, 0.0) - jnp.log(1.0 + jnp.exp(-jnp.abs(ft)))
    upper = (lax.broadcasted_iota(I32, (tm, tm), 0) <= lax.broadcasted_iota(I32, (tm, tm), 1)).astype(F32)
    c = jnp.dot(logf, upper, preferred_element_type=F32, precision=lax.Precision.HIGHEST) + carry_ref[...]
    c_ref[...] = c * LOG2E
    carry_ref[...] = c[:, tm - 1:tm]


def _fox_attn_kernel(q_ref, k_ref, v_ref, c_ref, o_ref, m_ref, l_ref, acc_ref, *, tq, sq, tk):
    ns = tq // sq
    q0 = pl.program_id(2) * tq
    m_ref[...] = jnp.full_like(m_ref, -jnp.inf)
    l_ref[...] = jnp.zeros_like(l_ref)
    acc_ref[...] = jnp.zeros_like(acc_ref)

    def stream(a, blocks):
        q = q_ref[a * sq:(a + 1) * sq, :]
        for start, size, tri in blocks:
            kb = k_ref[pl.ds(start, size), :]
            vb = v_ref[pl.ds(start, size), :]
            s = lax.dot_general(q, kb, NT_DIMS, preferred_element_type=F32) - c_ref[:, pl.ds(start, size)]
            yield
            if tri:
                s = jnp.where(lax.broadcasted_iota(I32, (sq, size), 0) >= lax.broadcasted_iota(I32, (sq, size), 1),
                              s, -jnp.inf)
            m_old = m_ref[a]
            m_new = jnp.maximum(m_old, jnp.max(s, axis=-1, keepdims=True))
            alpha = jnp.exp2(m_old - m_new)
            p = jnp.exp2(s - jnp.tile(m_new, (1, size // LANES)))
            l_new = l_ref[a] * alpha + jnp.sum(p, axis=-1, keepdims=True)
            pv = jnp.dot(p.astype(BF16), vb, preferred_element_type=F32)
            yield
            m_ref[a] = m_new
            l_ref[a] = l_new
            acc_ref[a] = acc_ref[a] * alpha + pv

    def body(j, carry):
        start = pl.multiple_of(j * tk, tk)
        _round_robin([stream(a, [(start, tk, False)]) for a in range(ns)])
        return carry

    lax.fori_loop(0, q0 // tk, body, 0)

    diag = []
    for a in range(ns):
        blocks, off = [], 0
        while off < a * sq:
            size = min(tk, a * sq - off)
            blocks.append((pl.multiple_of(q0 + off, sq), size, False))
            off += size
        blocks.append((pl.multiple_of(q0 + a * sq, sq), sq, True))
        diag.append(stream(a, blocks))
    _round_robin(diag)
    for a in range(ns):
        o_ref[a * sq:(a + 1) * sq, :] = (acc_ref[a] / l_ref[a]).astype(BF16)


def _fox_out_kernel(o_ref, og_ref, h_ref, wout_ref, out_ref):
    y = o_ref[...].astype(F32) * _sigmoid(og_ref[...].astype(F32))
    out_ref[...] = h_ref[...] + _bdot(y, wout_ref[...])


def _fox_layer(h, g, w_in, b_f, qn_g, kn_g, w_out, B, T):
    N, D = h.shape
    nh = FOX_HEADS
    hd = w_out.shape[0]
    dh = hd // nh
    tm = TM_FOX_IN
    nt = T // tm
    row = lambda b, i: (b * nt + i, 0)
    const2 = lambda b, i: (0, 0)
    q, k, v, og, c = pl.pallas_call(
        functools.partial(_fox_in_kernel, tm=tm, hd=hd, nh=nh),
        grid=(B, nt),
        in_specs=[
            pl.BlockSpec((tm, D), row),
            pl.BlockSpec((1, D), const2),
            pl.BlockSpec((D, 4 * hd), const2),
            pl.BlockSpec((nh, D), const2),
            pl.BlockSpec((nh, 1), const2),
            pl.BlockSpec((1, hd), const2),
            pl.BlockSpec((1, hd), const2),
        ],
        out_specs=[
            pl.BlockSpec((tm, hd), row),
            pl.BlockSpec((tm, hd), row),
            pl.BlockSpec((tm, hd), row),
            pl.BlockSpec((tm, hd), row),
            pl.BlockSpec((None, nh, tm), lambda b, i: (b, 0, i)),
        ],
        out_shape=[
            jax.ShapeDtypeStruct((N, hd), BF16),
            jax.ShapeDtypeStruct((N, hd), BF16),
            jax.ShapeDtypeStruct((N, hd), BF16),
            jax.ShapeDtypeStruct((N, hd), BF16),
            jax.ShapeDtypeStruct((B, nh, T), F32),
        ],
        scratch_shapes=[pltpu.VMEM((nh, 1), F32)],
        compiler_params=_cparams("parallel", "arbitrary"),
        name="fox_in",
    )(h, g.reshape(1, D), w_in[:, :4 * hd].astype(BF16), w_in[:, 4 * hd:].T.astype(BF16),
      b_f.astype(F32).reshape(nh, 1), jnp.tile(qn_g.astype(F32), nh).reshape(1, hd),
      jnp.tile(kn_g.astype(F32), nh).reshape(1, hd))

    tq, sq, tk = FOX_TQ, FOX_SQ, FOX_TK
    assert dh == LANES and T % tq == 0 and tq % sq == 0 and tq % tk == 0
    nq = T // tq
    o = pl.pallas_call(
        functools.partial(_fox_attn_kernel, tq=tq, sq=sq, tk=tk),
        grid=(B, nh, nq),
        in_specs=[
            pl.BlockSpec((tq, dh), lambda b, hh, i: (b * nq + i, hh)),
            pl.BlockSpec((T, dh), lambda b, hh, i: (b, hh)),
            pl.BlockSpec((T, dh), lambda b, hh, i: (b, hh)),
            pl.BlockSpec((None, 1, T), lambda b, hh, i: (b * nh + hh, 0, 0)),
        ],
        out_specs=pl.BlockSpec((tq, dh), lambda b, hh, i: (b * nq + i, hh)),
        out_shape=jax.ShapeDtypeStruct((N, hd), BF16),
        scratch_shapes=[pltpu.VMEM((tq // sq, sq, LANES), F32)] * 3,
        compiler_params=_cparams("parallel", "parallel", "arbitrary"),
        name="fox_attention",
    )(q, k, v, c.reshape(B * nh, 1, T))

    row1 = lambda i: (i, 0)
    const1 = lambda i: (0, 0)
    to = TM_OUT
    return pl.pallas_call(
        _fox_out_kernel,
        grid=(N // to,),
        in_specs=[
            pl.BlockSpec((to, hd), row1),
            pl.BlockSpec((to, hd), row1),
            pl.BlockSpec((to, D), row1),
            pl.BlockSpec((hd, D), const1),
        ],
        out_specs=pl.BlockSpec((to, D), row1),
        out_shape=jax.ShapeDtypeStruct((N, D), F32),
        compiler_params=_cparams("parallel"),
        name="fox_out",
    )(o, og, h, w_out.astype(BF16))


def _router_kernel(h_ref, g_ref, wt_ref, bt_ref, earlier_ref, xs_ref, meta_ref, cnt_ref, run_ref, *, tm, d):
    i = pl.program_id(0)

    @pl.when(i == 0)
    def _():
        run_ref[...] = jnp.zeros_like(run_ref)

    xn = _rms(h_ref[...], g_ref[...])
    x_hi = xn.astype(BF16)
    x_hi32 = x_hi.astype(F32)
    x_lo = (xn - x_hi32).astype(BF16)
    nt = lambda a, b: lax.dot_general(a, b, NT_DIMS, preferred_element_type=F32)
    logits = (nt(wt_ref[0], x_hi) + nt(wt_ref[0], x_lo) + nt(wt_ref[1], x_hi)) + bt_ref[...]
    col = lambda j: logits[j:j + 1, :]
    gl = [col(j) for j in range(MOE_GROUPS)]
    gmax = functools.reduce(jnp.maximum, gl)
    gidx = jnp.full_like(gmax, MOE_GROUPS - 1).astype(I32)
    for j in range(MOE_GROUPS - 2, -1, -1):
        gidx = jnp.where(gl[j] == gmax, j, gidx)
    grp_p = 1.0 / functools.reduce(jnp.add, [jnp.exp(l - gmax) for l in gl])
    ev = []
    for kk in range(MOE_PER_GROUP):
        e = col(MOE_GROUPS + (MOE_GROUPS - 1) * MOE_PER_GROUP + kk)
        for gg in range(MOE_GROUPS - 2, -1, -1):
            e = jnp.where(gidx == gg, col(MOE_GROUPS + gg * MOE_PER_GROUP + kk), e)
        ev.append(e)
    m1 = functools.reduce(jnp.maximum, ev)
    i1 = jnp.full_like(gidx, MOE_PER_GROUP - 1)
    for kk in range(MOE_PER_GROUP - 2, -1, -1):
        i1 = jnp.where(ev[kk] == m1, kk, i1)
    ev2 = [jnp.where(i1 == kk, -jnp.inf, ev[kk]) for kk in range(MOE_PER_GROUP)]
    m2 = functools.reduce(jnp.maximum, ev2)
    i2 = jnp.full_like(gidx, MOE_PER_GROUP - 1)
    for kk in range(MOE_PER_GROUP - 2, -1, -1):
        i2 = jnp.where((ev2[kk] == m2) & (i1 != kk), kk, i2)
    i2 = jnp.where((i1 == MOE_PER_GROUP - 1) & (i2 == MOE_PER_GROUP - 1), MOE_PER_GROUP - 2, i2)
    e2 = jnp.exp(m2 - m1)
    w1 = grp_p / (1.0 + e2)
    w2 = grp_p * e2 / (1.0 + e2)
    lo = jnp.minimum(i1, i2)
    hi = jnp.maximum(i1, i2)
    w_lo = jnp.where(i1 < i2, w1, w2)
    w_hi = jnp.where(i1 < i2, w2, w1)
    base = jnp.where(lo == 0, 0, jnp.where(lo == 1, 3, 5))
    cls = gidx * MOE_PAIRS + base + hi - lo - 1
    srow = lax.broadcasted_iota(I32, (CLASS_ROWS, tm), 0)
    onehot = (srow == cls).astype(F32)
    before = jnp.dot(onehot.astype(BF16), earlier_ref[...], preferred_element_type=F32)
    run = run_ref[...]
    rank = jnp.sum(onehot * (before + run), axis=0, keepdims=True)
    run = run + jnp.sum(onehot, axis=1, keepdims=True)
    run_ref[...] = run
    cnt_ref[...] = jnp.broadcast_to(run, cnt_ref.shape)
    zero = jnp.zeros_like(w_lo)
    meta_ref[...] = jnp.concatenate([w_lo, w_hi, cls.astype(F32), rank, zero, zero, zero, zero], axis=0)
    wrows = jnp.concatenate([w_lo, w_hi, jnp.zeros((LANES - 2, tm), F32)], axis=0)
    wbits = pltpu.bitcast(jnp.transpose(wrows), jnp.uint32)
    xs_ref[...] = jnp.concatenate([_pack_rounded_pairs(x_hi32), wbits], axis=-1)


def _pack_rounded_pairs(xr):
    half = xr.shape[1] // 2
    bits = pltpu.bitcast(xr, jnp.uint32)
    return lax.shift_right_logical(bits[:, :half], jnp.uint32(16)) | bits[:, half:]


def _pack_bf16_pairs(x):
    return _pack_rounded_pairs(x.astype(BF16).astype(F32))


def _unpack_bf16_pairs(word):
    lo = pltpu.bitcast(lax.shift_left(word, jnp.uint32(16)), F32)
    hi = pltpu.bitcast(word & jnp.uint32(0xFFFF0000), F32)
    return jnp.concatenate([lo, hi], axis=-1)


def _unpack_rows(x, half):
    gates = pltpu.bitcast(x[:, half:], F32)
    return _unpack_bf16_pairs(x[:, :half]).astype(BF16), gates[:, 0:1], gates[:, 1:2]


def _position_kernel(meta_ref, off_ref, pos_ref):
    cls = meta_ref[2:3, :].astype(I32)
    srow = lax.broadcasted_iota(I32, (CLASS_ROWS, cls.shape[1]), 0)
    off = jnp.sum(jnp.where(srow == cls, off_ref[...], 0.0), axis=0, keepdims=True)
    pos_ref[...] = (off + meta_ref[3:4, :]).astype(I32)


def _dispatch_kernel(pos_ref, xs_ref, out_ref, sem, *, tm):
    copies = []
    for r in range(tm):
        pos = pos_ref[0, 0, r]
        cp = pltpu.make_async_copy(xs_ref.at[pl.ds(r, 1), :], out_ref.at[pl.ds(pos, 1), :], sem)
        cp.start(priority=r % 2)
        copies.append(cp)
    for cp in copies:
        cp.wait()


def _expert_kernel(ea_ref, eb_ref, nv_ref, x_ref, wgua_ref, wda_ref, wgub_ref, wdb_ref, y_ref, *, d, dff):
    j = pl.program_id(0)

    @pl.when(j < nv_ref[0])
    def _():
        xn, wa, wb = _unpack_rows(x_ref[...], d // 2)

        hgu = [jnp.dot(xn, w_ref[...], preferred_element_type=F32) for w_ref in (wgua_ref, wgub_ref)]
        act = [(_silu(hh[:, :dff]) * hh[:, dff:]).astype(BF16) for hh in hgu]
        ya, yb = [jnp.dot(a, w_ref[...], preferred_element_type=F32) for a, w_ref in zip(act, (wda_ref, wdb_ref))]
        y_ref[...] = _pack_bf16_pairs(wa * ya + wb * yb)


def _combine_kernel(pos_ref, h_ref, ys_ref, out_ref, buf_ref, sem, *, tm):
    copies = []
    for r in range(tm):
        pos = pos_ref[0, 0, r]
        cp = pltpu.make_async_copy(ys_ref.at[pl.ds(pos, 1), :], buf_ref.at[pl.ds(r, 1), :], sem)
        cp.start(priority=r % 2)
        copies.append(cp)
    for cp in copies:
        cp.wait()
    out_ref[...] = h_ref[...] + _unpack_bf16_pairs(buf_ref[...])


def _moe_layer(h, g, w_group, b_group, w_router, b_router, w_gate, w_up, w_down):
    N, D = h.shape
    ne, _, dff = w_gate.shape
    tm = TM_ROUTE
    nb = N // tm
    n_log = MOE_GROUPS + ne
    assert n_log <= CLASS_ROWS
    w_rt = jnp.pad(jnp.concatenate([w_group, w_router], axis=1).astype(F32).T, ((0, CLASS_ROWS - n_log), (0, 0)))
    w_rt_hi = w_rt.astype(BF16)
    w_rt = jnp.stack([w_rt_hi, (w_rt - w_rt_hi.astype(F32)).astype(BF16)])
    b_rt = jnp.pad(jnp.concatenate([b_group, b_router]).astype(F32), (0, CLASS_ROWS - n_log)).reshape(CLASS_ROWS, 1)
    DX = D // 2 + LANES
    row1 = lambda i: (i, 0)
    const1 = lambda i: (0, 0)
    xs, meta, cnt = pl.pallas_call(
        functools.partial(_router_kernel, tm=tm, d=D),
        grid=(nb,),
        in_specs=[
            pl.BlockSpec((tm, D), row1),
            pl.BlockSpec((1, D), const1),
            pl.BlockSpec((2, CLASS_ROWS, D), lambda i: (0, 0, 0)),
            pl.BlockSpec((CLASS_ROWS, 1), const1),
            pl.BlockSpec((tm, tm), const1),
        ],
        out_specs=[
            pl.BlockSpec((tm, DX), row1),
            pl.BlockSpec((8, tm), lambda i: (0, i)),
            pl.BlockSpec((CLASS_ROWS, LANES), const1),
        ],
        out_shape=[
            jax.ShapeDtypeStruct((N, DX), jnp.uint32),
            jax.ShapeDtypeStruct((8, N), F32),
            jax.ShapeDtypeStruct((CLASS_ROWS, LANES), F32),
        ],
        scratch_shapes=[pltpu.VMEM((CLASS_ROWS, 1), F32)],
        compiler_params=_cparams("arbitrary"),
        name="moe_router",
    )(h, g.reshape(1, D), w_rt, b_rt, (jnp.arange(tm)[:, None] < jnp.arange(tm)[None, :]).astype(BF16))

    counts = cnt[:MOE_CLASSES, 0].astype(I32)
    tiles_per = (counts + TE - 1) // TE
    tile_end = jnp.cumsum(tiles_per)
    offsets = jnp.pad((tile_end - tiles_per) * TE, (0, CLASS_ROWS - MOE_CLASSES))
    n_tiles = N // TE + MOE_CLASSES
    n_valid = tile_end[-1]
    tile_id = jnp.minimum(jnp.arange(n_tiles, dtype=I32), n_valid - 1)
    tile_cls = jnp.sum((tile_end[None, :] <= tile_id[:, None]).astype(I32), axis=1)
    grp = tile_cls // MOE_PAIRS
    pid = tile_cls % MOE_PAIRS
    tile_ea = (grp * MOE_PER_GROUP + jnp.asarray(PAIR_LO, I32)[pid]).astype(I32)
    tile_eb = (grp * MOE_PER_GROUP + jnp.asarray(PAIR_HI, I32)[pid]).astype(I32)
    tp = TM_PERMUTE
    npb = N // tp
    tpos = min(8192, N)
    pos3 = pl.pallas_call(
        _position_kernel,
        grid=(N // tpos,),
        in_specs=[pl.BlockSpec((8, tpos), lambda i: (0, i)), pl.BlockSpec((CLASS_ROWS, 1), const1)],
        out_specs=pl.BlockSpec((1, tpos), lambda i: (0, i)),
        out_shape=jax.ShapeDtypeStruct((1, N), I32),
        compiler_params=_cparams("parallel"),
        name="moe_positions",
    )(meta, offsets.astype(F32).reshape(CLASS_ROWS, 1)).reshape(npb, 1, tp)

    P = n_tiles * TE
    smem3 = pl.BlockSpec((1, 1, tp), lambda i: (i, 0, 0), memory_space=pltpu.SMEM)
    xsort = pl.pallas_call(
        functools.partial(_dispatch_kernel, tm=tp),
        grid=(npb,),
        in_specs=[smem3, pl.BlockSpec((tp, DX), row1)],
        out_specs=pl.BlockSpec(memory_space=pl.ANY),
        out_shape=jax.ShapeDtypeStruct((P, DX), jnp.uint32),
        scratch_shapes=[pltpu.SemaphoreType.DMA(())],
        compiler_params=_cparams("arbitrary"),
        name="moe_dispatch",
    )(pos3, xs)

    w_gu = jnp.concatenate([w_gate, w_up], axis=-1).astype(BF16)
    w_dn = w_down.astype(BF16)
    ysort = pl.pallas_call(
        functools.partial(_expert_kernel, d=D, dff=dff),
        grid_spec=pltpu.PrefetchScalarGridSpec(
            num_scalar_prefetch=3,
            grid=(n_tiles,),
            in_specs=[
                pl.BlockSpec((TE, DX), lambda j, ea, eb, nv: (j, 0)),
                pl.BlockSpec((None, D, 2 * dff), lambda j, ea, eb, nv: (ea[j], 0, 0)),
                pl.BlockSpec((None, dff, D), lambda j, ea, eb, nv: (ea[j], 0, 0)),
                pl.BlockSpec((None, D, 2 * dff), lambda j, ea, eb, nv: (eb[j], 0, 0)),
                pl.BlockSpec((None, dff, D), lambda j, ea, eb, nv: (eb[j], 0, 0)),
            ],
            out_specs=pl.BlockSpec((TE, D // 2), lambda j, ea, eb, nv: (j, 0)),
        ),
        out_shape=jax.ShapeDtypeStruct((P, D // 2), jnp.uint32),
        compiler_params=_cparams("arbitrary"),
        name="moe_experts",
    )(tile_ea, tile_eb, n_valid.reshape(1).astype(I32), xsort, w_gu, w_dn, w_gu, w_dn)

    return pl.pallas_call(
        functools.partial(_combine_kernel, tm=tp),
        grid=(npb,),
        in_specs=[smem3, pl.BlockSpec((tp, D), row1), pl.BlockSpec(memory_space=pl.ANY)],
        out_specs=pl.BlockSpec((tp, D), row1),
        out_shape=jax.ShapeDtypeStruct((N, D), F32),
        scratch_shapes=[pltpu.VMEM((tp, D // 2), jnp.uint32), pltpu.SemaphoreType.DMA(())],
        compiler_params=_cparams("arbitrary"),
        name="moe_combine",
    )(pos3, h, ysort)


def kernel(x, norm_mix_g, norm_ffn_g, pool_w_in, pool_w_grp, pool_scale, gdn_w_in, gdn_conv_w, gdn_a_log, gdn_dt_bias, gdn_norm_g, gdn_w_out, fox_w_in, fox_b_f, fox_q_norm_g, fox_k_norm_g, fox_w_out, moe_w_group, moe_b_group, moe_w_router, moe_b_router, moe_w_gate, moe_w_up, moe_w_down):
    B, T, D = x.shape
    depth = norm_mix_g.shape[0]
    h = x.reshape(B * T, D)
    for i in range(depth):
        m, slot = i % 3, i // 3
        if m == 0:
            h = _pool_layer(h, norm_mix_g[i], pool_w_in[slot], pool_w_grp[slot], pool_scale[slot], B, T)
        elif m == 1:
            h = _gdn_layer(h, norm_mix_g[i], gdn_w_in[slot], gdn_conv_w[slot], gdn_a_log[slot],
                           gdn_dt_bias[slot], gdn_norm_g[slot], gdn_w_out[slot], B, T)
        else:
            h = _fox_layer(h, norm_mix_g[i], fox_w_in[slot], fox_b_f[slot], fox_q_norm_g[slot],
                           fox_k_norm_g[slot], fox_w_out[slot], B, T)
        h = _moe_layer(h, norm_ffn_g[i], moe_w_group[i], moe_b_group[i], moe_w_router[i], moe_b_router[i],
                       moe_w_gate[i], moe_w_up[i], moe_w_down[i])
    return h.reshape(B, T, D)
```
